```python
import jax
import jax.numpy as jnp
from jax import lax
import numpy as np

D_MODEL = 1024
BATCH = 4
SEQ = 8192
DEPTH = 2

D_MIX = D_MODEL
SB_HEAD_DIM = 64
D_SB = D_MIX // 2
SB_HEADS = D_SB // SB_HEAD_DIM
BLOCK_Q = 128
D_LRU = D_MIX // 4
LRU_BLOCKS = 4
LRU_BLOCK = D_LRU // LRU_BLOCKS
LRU_CONV_WIDTH = 4
LRU_C = 8.0
D_CONF = D_MIX // 4
CONF_CONV_WIDTH = 31
D_IN = 3 * D_SB + 2 * D_LRU + 2 * D_CONF
N_GROUPS = 4
EXPERTS_PER_GROUP = 8
N_EXPERTS = N_GROUPS * EXPERTS_PER_GROUP
TOP_K_FINE = 2
D_EXPERT = D_MODEL // 2
MOE_BLOCK = 256
EPS = 1e-6

kernel_name = 'hybrid_stickbreak_rglru_conformer_hmoe'


def rms_norm(x, g):
    xf = x.astype(jnp.float32)
    y = xf * lax.rsqrt(jnp.mean(xf * xf, axis=-1, keepdims=True) + EPS)
    return (y * g.astype(jnp.float32)).astype(x.dtype)


def layer_norm(x, g, b):
    xf = x.astype(jnp.float32)
    mu = jnp.mean(xf, axis=-1, keepdims=True)
    xc = xf - mu
    var = jnp.mean(xc * xc, axis=-1, keepdims=True)
    return xc * lax.rsqrt(var + EPS) * g.astype(jnp.float32) + b.astype(jnp.float32)


def causal_depthwise_conv(x, w, b):
    k_width, chans = w.shape
    y = lax.conv_general_dilated(
        x, w.astype(x.dtype)[:, None, :], window_strides=(1,),
        padding=[(k_width - 1, 0)], dimension_numbers=('NWC', 'WIO', 'NWC'),
        feature_group_count=chans)
    return y + b.astype(x.dtype)


def stick_breaking_attention(q, k, v):
    seq = q.shape[2]
    scale = SB_HEAD_DIM ** -0.5
    outs = []
    for i in range(seq // BLOCK_Q):
        q0 = i * BLOCK_Q
        kv_len = q0 + BLOCK_Q
        qb = q[:, :, q0:kv_len]
        kb = k[:, :, :kv_len]
        vb = v[:, :, :kv_len]
        z = jnp.einsum('bhqd,bhkd->bhqk', qb, kb, preferred_element_type=jnp.float32) * scale
        t_pos = q0 + jnp.arange(BLOCK_Q, dtype=jnp.int32)[:, None]
        s_pos = jnp.arange(kv_len, dtype=jnp.int32)[None, :]
        mask = s_pos < t_pos
        log_keep = jnp.where(mask, jax.nn.log_sigmoid(-z), 0.0)
        suffix = lax.cumsum(log_keep, axis=3, reverse=True) - log_keep
        wts = jnp.where(mask, jnp.exp(jax.nn.log_sigmoid(z) + suffix), 0.0)
        outs.append(jnp.einsum('bhqk,bhkd->bhqd', wts.astype(vb.dtype), vb))
    return jnp.concatenate(outs, axis=2)


def _linear_combine(c1, c2):
    a1, b1 = c1
    a2, b2 = c2
    return a1 * a2, a2 * b1 + b2


def rg_lru_branch(xr, xg, conv_w, conv_b, wa, ba, wx, bx, lam):
    f32 = jnp.float32
    bsz, seq, _ = xr.shape
    xr = causal_depthwise_conv(xr, conv_w, conv_b)
    xblk = xr.reshape(bsz, seq, LRU_BLOCKS, LRU_BLOCK)
    gate_r = jnp.einsum('bsnc,ncd->bsnd', xblk, wa).reshape(bsz, seq, D_LRU) + ba
    gate_i = jnp.einsum('bsnc,ncd->bsnd', xblk, wx).reshape(bsz, seq, D_LRU) + bx
    log_a = LRU_C * jax.nn.sigmoid(gate_r.astype(f32)) * jax.nn.log_sigmoid(lam.astype(f32))
    a = jnp.exp(log_a)
    u = jnp.sqrt(-jnp.expm1(2.0 * log_a)) * (jax.nn.sigmoid(gate_i.astype(f32)) * xr.astype(f32))
    _, h = lax.associative_scan(_linear_combine, (a, u), axis=1)
    return (h * jax.nn.gelu(xg.astype(f32))).astype(xr.dtype)


def conformer_conv(xc, dw_w, dw_b, ln_g, ln_b, pw_w, pw_b):
    val, gate = jnp.split(xc, 2, axis=-1)
    u = val * jax.nn.sigmoid(gate)
    u = causal_depthwise_conv(u, dw_w, dw_b)
    u = jax.nn.silu(layer_norm(u, ln_g, ln_b)).astype(xc.dtype)
    return jnp.einsum('bsc,ce->bse', u, pw_w) + pw_b


def hierarchical_moe(h, w_coarse, w_fine, w_gate, w_up, w_down):
    n_tok, d = h.shape
    f32 = jnp.float32
    logits_c = jnp.einsum('nd,dg->ng', h, w_coarse, preferred_element_type=f32)
    probs_c = jax.nn.softmax(logits_c, axis=-1)
    grp = jnp.argmax(logits_c, axis=-1).astype(jnp.int32)
    w_grp = jnp.take_along_axis(probs_c, grp[:, None], axis=-1)
    logits_f = jnp.einsum('nd,de->ne', h, w_fine, preferred_element_type=f32)
    logits_f = logits_f.reshape(n_tok, N_GROUPS, EXPERTS_PER_GROUP)
    logits_f = jnp.take_along_axis(logits_f, grp[:, None, None], axis=1)[:, 0]
    top_val, top_idx = lax.top_k(logits_f, TOP_K_FINE)
    gate = (jax.nn.softmax(top_val, axis=-1) * w_grp).astype(h.dtype)
    expert = grp[:, None] * EXPERTS_PER_GROUP + top_idx.astype(jnp.int32)

    n_asg = n_tok * TOP_K_FINE
    flat_e = expert.reshape(n_asg)
    flat_tok = jnp.repeat(jnp.arange(n_tok, dtype=jnp.int32), TOP_K_FINE)
    flat_gate = gate.reshape(n_asg)
    order = jnp.argsort(flat_e)
    sorted_e = flat_e[order]
    counts = jnp.bincount(flat_e, length=N_EXPERTS).astype(jnp.int32)
    padded = (counts + MOE_BLOCK - 1) // MOE_BLOCK * MOE_BLOCK
    starts = jnp.cumsum(counts) - counts
    padded_ends = jnp.cumsum(padded)
    padded_starts = padded_ends - padded
    dest = padded_starts[sorted_e] + jnp.arange(n_asg, dtype=jnp.int32) - starts[sorted_e]
    n_blocks = -(-n_asg // MOE_BLOCK) + N_EXPERTS
    n_slots = n_blocks * MOE_BLOCK
    slot_tok = jnp.full((n_slots,), n_tok, jnp.int32).at[dest].set(flat_tok[order])
    slot_gate = jnp.zeros((n_slots,), h.dtype).at[dest].set(flat_gate[order])
    block_start = jnp.arange(n_blocks, dtype=jnp.int32) * MOE_BLOCK
    block_e = jnp.minimum(jnp.searchsorted(padded_ends, block_start, side='right'),
                          N_EXPERTS - 1).astype(jnp.int32)
    h_pad = jnp.concatenate([h, jnp.zeros((1, d), h.dtype)], axis=0)
    xb = h_pad[slot_tok].reshape(n_blocks, MOE_BLOCK, d)

    def run_block(args):
        xe, e = args
        return (jax.nn.silu(xe @ w_gate[e]) * (xe @ w_up[e])) @ w_down[e]

    yb = lax.map(run_block, (xb, block_e)).reshape(n_slots, d)
    y = jnp.zeros((n_tok + 1, d), yb.dtype).at[slot_tok].add(yb * slot_gate[:, None])
    return y[:n_tok]


def setup_inputs(seed: int = 0) -> dict:
    key = jax.random.key(seed)
    ks = jax.random.split(key, 28)
    f32 = jnp.float32
    L = DEPTH

    def nrm(k, shape, scale):
        return jax.random.normal(k, shape, f32) * scale

    def gain(k, shape):
        return 1.0 + 0.05 * jax.random.normal(k, shape, f32)

    a_c = jax.random.uniform(ks[11], (L, D_LRU), f32, 0.9, 0.999)
    a0 = a_c ** (1.0 / LRU_C)
    lam = jnp.log(a0) - jnp.log1p(-a0)

    return {
        'x': jax.random.normal(ks[0], (BATCH, SEQ, D_MODEL), f32),
        'norm1_g': gain(ks[1], (L, D_MODEL)),
        'w_in': nrm(ks[2], (L, D_MODEL, D_IN), D_MODEL ** -0.5),
        'q_norm_g': gain(ks[3], (L, SB_HEAD_DIM)),
        'k_norm_g': gain(ks[4], (L, SB_HEAD_DIM)),
        'lru_conv_w': nrm(ks[5], (L, LRU_CONV_WIDTH, D_LRU), LRU_CONV_WIDTH ** -0.5),
        'lru_conv_b': nrm(ks[6], (L, D_LRU), 0.02),
        'lru_wa': nrm(ks[7], (L, LRU_BLOCKS, LRU_BLOCK, LRU_BLOCK), LRU_BLOCK ** -0.5),
        'lru_ba': nrm(ks[8], (L, D_LRU), 0.02),
        'lru_wx': nrm(ks[9], (L, LRU_BLOCKS, LRU_BLOCK, LRU_BLOCK), LRU_BLOCK ** -0.5),
        'lru_bx': nrm(ks[10], (L, D_LRU), 0.02),
        'lru_lambda': lam,
        'conf_dw_w': nrm(ks[12], (L, CONF_CONV_WIDTH, D_CONF), CONF_CONV_WIDTH ** -0.5),
        'conf_dw_b': nrm(ks[13], (L, D_CONF), 0.02),
        'conf_ln_g': gain(ks[14], (L, D_CONF)),
        'conf_ln_b': nrm(ks[15], (L, D_CONF), 0.02),
        'conf_pw_w': nrm(ks[16], (L, D_CONF, D_CONF), D_CONF ** -0.5),
        'conf_pw_b': nrm(ks[17], (L, D_CONF), 0.02),
        'out_norm_g': gain(ks[18], (L, D_MIX)),
        'w_out': nrm(ks[19], (L, D_MIX, D_MODEL), D_MIX ** -0.5),
        'norm2_g': gain(ks[20], (L, D_MODEL)),
        'router_coarse': nrm(ks[21], (L, D_MODEL, N_GROUPS), D_MODEL ** -0.5),
        'router_fine': nrm(ks[22], (L, D_MODEL, N_EXPERTS), D_MODEL ** -0.5),
        'exp_w_gate': nrm(ks[23], (L, N_EXPERTS, D_MODEL, D_EXPERT), D_MODEL ** -0.5),
        'exp_w_up': nrm(ks[24], (L, N_EXPERTS, D_MODEL, D_EXPERT), D_MODEL ** -0.5),
        'exp_w_down': nrm(ks[25], (L, N_EXPERTS, D_EXPERT, D_MODEL), D_EXPERT ** -0.5),
    }


def reference(x, norm1_g, w_in, q_norm_g, k_norm_g, lru_conv_w, lru_conv_b, lru_wa, lru_ba,
              lru_wx, lru_bx, lru_lambda, conf_dw_w, conf_dw_b, conf_ln_g, conf_ln_b,
              conf_pw_w, conf_pw_b, out_norm_g, w_out, norm2_g, router_coarse, router_fine,
              exp_w_gate, exp_w_up, exp_w_down):
    bsz, seq, d = x.shape
    splits = [D_SB, 2 * D_SB, 3 * D_SB, 3 * D_SB + D_LRU, 3 * D_SB + 2 * D_LRU]
    for l in range(DEPTH):
        h = rms_norm(x, norm1_g[l])
        proj = jnp.einsum('bsd,de->bse', h, w_in[l])
        q, k, v, xr, xg, xc = jnp.split(proj, splits, axis=-1)
        q = rms_norm(q.reshape(bsz, seq, SB_HEADS, SB_HEAD_DIM), q_norm_g[l]).transpose(0, 2, 1, 3)
        k = rms_norm(k.reshape(bsz, seq, SB_HEADS, SB_HEAD_DIM), k_norm_g[l]).transpose(0, 2, 1, 3)
        v = v.reshape(bsz, seq, SB_HEADS, SB_HEAD_DIM).transpose(0, 2, 1, 3)
        y_a = stick_breaking_attention(q, k, v).transpose(0, 2, 1, 3).reshape(bsz, seq, D_SB)
        y_b = rg_lru_branch(xr, xg, lru_conv_w[l], lru_conv_b[l], lru_wa[l], lru_ba[l],
                            lru_wx[l], lru_bx[l], lru_lambda[l])
        y_c = conformer_conv(xc, conf_dw_w[l], conf_dw_b[l], conf_ln_g[l], conf_ln_b[l],
                             conf_pw_w[l], conf_pw_b[l])
        g_out = out_norm_g[l]
        y = jnp.concatenate([
            rms_norm(y_a, g_out[:D_SB]),
            rms_norm(y_b, g_out[D_SB:D_SB + D_LRU]),
            rms_norm(y_c, g_out[D_SB + D_LRU:]),
        ], axis=-1)
        x = x + jnp.einsum('bse,ed->bsd', y, w_out[l])
        h2 = rms_norm(x, norm2_g[l]).reshape(bsz * seq, d)
        x = x + hierarchical_moe(h2, router_coarse[l], router_fine[l], exp_w_gate[l],
                                 exp_w_up[l], exp_w_down[l]).reshape(bsz, seq, d)
    return x
```

```python
import functools

import jax
import jax.numpy as jnp
from jax import lax
from jax.experimental import pallas as pl
from jax.experimental.pallas import tpu as pltpu

F32 = jnp.float32
BF16 = jnp.bfloat16

D_MODEL = 1024
HEAD_DIM = 64
D_SB = 512
D_LRU = 256
LRU_BLOCKS = 4
LRU_CONV_WIDTH = 4
LRU_C = 8.0
D_CONF = 256
CONF_CONV_WIDTH = 31
D_QKV = 3 * D_SB
D_REST = 2 * D_LRU + 2 * D_CONF
N_GROUPS = 4
EXPERTS_PER_GROUP = 8
N_EXPERTS = N_GROUPS * EXPERTS_PER_GROUP
D_EXPERT = 512
MOE_BLOCK = 256
EPS = 1e-6

LANES = 128
ROW_TILE = 256
ATTN_BLOCK = 128
MIX_TILE = 512
CONF_HALO = 32
LRU_HALO = 8
ATTN_DEAD_LOG = -110.0
VMEM_LIMIT = 48 * 1024 * 1024


def _cparams(*sem):
    return pltpu.CompilerParams(dimension_semantics=sem, vmem_limit_bytes=VMEM_LIMIT)


def _rms(y):
    return y * lax.rsqrt(jnp.mean(y * y, axis=-1, keepdims=True) + EPS)


def _inproj_kernel(x_ref, g_ref, w_ref, qg_ref, kg_ref, gsum_ref, qkv_ref, rest_ref):
    h = (_rms(x_ref[...]) * g_ref[...]).astype(BF16)

    def head_norm(p, gain):
        s = jnp.dot((p * p).astype(BF16), gsum_ref[...], preferred_element_type=F32)
        return p * lax.rsqrt(s * (1.0 / HEAD_DIM) + EPS) * gain

    q = jnp.dot(h, w_ref[:, 0:D_SB], preferred_element_type=F32)
    qkv_ref[:, 0:D_SB] = head_norm(q, qg_ref[...]).astype(BF16)
    k = jnp.dot(h, w_ref[:, D_SB:2 * D_SB], preferred_element_type=F32)
    qkv_ref[:, D_SB:2 * D_SB] = head_norm(k, kg_ref[...]).astype(BF16)
    v = jnp.dot(h, w_ref[:, 2 * D_SB:D_QKV], preferred_element_type=F32)
    qkv_ref[:, 2 * D_SB:D_QKV] = v.astype(BF16)
    rest_ref[...] = jnp.dot(h, w_ref[:, D_QKV:], preferred_element_type=F32)


def _inproj(x, g, w, qg, kg, gsum):
    n = x.shape[0]
    tm = ROW_TILE
    const = lambda i: (0, 0)
    return pl.pallas_call(
        _inproj_kernel,
        grid=(n // tm,),
        in_specs=[
            pl.BlockSpec((tm, D_MODEL), lambda i: (i, 0)),
            pl.BlockSpec((1, D_MODEL), const),
            pl.BlockSpec((D_MODEL, D_QKV + D_REST), const),
            pl.BlockSpec((1, D_SB), const),
            pl.BlockSpec((1, D_SB), const),
            pl.BlockSpec((D_SB, D_SB), const),
        ],
        out_specs=[
            pl.BlockSpec((tm, D_QKV), lambda i: (i, 0)),
            pl.BlockSpec((tm, D_REST), lambda i: (i, 0)),
        ],
        out_shape=[
            jax.ShapeDtypeStruct((n, D_QKV), BF16),
            jax.ShapeDtypeStruct((n, D_REST), F32),
        ],
        compiler_params=_cparams("arbitrary"),
        name="inproj",
    )(x, g, w, qg, kg, gsum)


def _attn_kernel(q_ref, k_ref, v_ref, o_ref):
    bq = bk = ATTN_BLOCK
    i = pl.program_id(2)
    q = q_ref[0]
    lane = lax.broadcasted_iota(jnp.int32, (bq, LANES), 1)
    r_io = lax.broadcasted_iota(jnp.int32, (bq, bk), 0)
    c_io = lax.broadcasted_iota(jnp.int32, (bq, bk), 1)
    diag_mask = c_io < r_io
    later = (r_io > c_io).astype(BF16)
    scale = HEAD_DIM ** -0.5

    def block(qh, j, carry, acc, masked):
        ks = pl.multiple_of(j * bk, bk)
        kb = k_ref[0, pl.ds(ks, bk), :]
        vb = v_ref[0, pl.ds(ks, bk), :]
        z = lax.dot_general(qh, kb, (((1,), (1,)), ((), ())), preferred_element_type=F32)
        l1p = jnp.log1p(jnp.exp(-jnp.abs(z)))
        log_keep = -(jnp.maximum(z, 0.0) + l1p)
        log_beta = jnp.minimum(z, 0.0) - l1p
        if masked:
            log_keep = jnp.where(diag_mask, log_keep, 0.0)
        hi = log_keep.astype(BF16)
        lo = (log_keep - hi.astype(F32)).astype(BF16)
        within = (jnp.dot(hi, later, preferred_element_type=F32)
                  + jnp.dot(lo, later, preferred_element_type=F32))
        w = jnp.exp(log_beta + within + carry)
        if masked:
            w = jnp.where(diag_mask, w, 0.0)
        acc = acc + jnp.dot(w.astype(BF16), vb, preferred_element_type=F32)
        carry = carry + jnp.sum(log_keep, axis=-1, keepdims=True)
        return carry, acc

    def one_head(head_lanes):
        qh = jnp.where(head_lanes, q, jnp.zeros_like(q)) * jnp.asarray(scale, BF16)
        carry = jnp.zeros((bq, 1), F32)
        acc = jnp.zeros((bq, LANES), F32)
        carry, acc = block(qh, i, carry, acc, True)

        def cond(st):
            j, carry, _ = st
            return jnp.logical_and(j >= 0, jnp.max(carry) > ATTN_DEAD_LOG)

        def body(st):
            j, carry, acc = st
            carry, acc = block(qh, j, carry, acc, False)
            return j - 1, carry, acc

        _, _, acc = lax.while_loop(cond, body, (i - 1, carry, acc))
        return acc

    first = lane < HEAD_DIM
    acc0 = one_head(first)
    acc1 = one_head(jnp.logical_not(first))
    o_ref[0] = jnp.where(first, acc0, acc1)


def _attention(qkv, bsz, seq):
    bq = ATTN_BLOCK
    pairs = D_SB // LANES
    qkv3 = qkv.reshape(bsz, seq, D_QKV)
    return pl.pallas_call(
        _attn_kernel,
        grid=(bsz, pairs, seq // bq),
        in_specs=[
            pl.BlockSpec((1, bq, LANES), lambda b, p, i: (b, i, p)),
            pl.BlockSpec((1, seq, LANES), lambda b, p, i: (b, 0, pairs + p)),
            pl.BlockSpec((1, seq, LANES), lambda b, p, i: (b, 0, 2 * pairs + p)),
        ],
        out_specs=pl.BlockSpec((1, bq, LANES), lambda b, p, i: (b, i, p)),
        out_shape=jax.ShapeDtypeStruct((bsz, seq, D_SB), F32),
        compiler_params=_cparams("arbitrary", "arbitrary", "arbitrary"),
        name="attn",
    )(qkv3, qkv3, qkv3)


def _shift_rows(x, d, fill):
    t = x.shape[0]
    if d % 8 == 0:
        return jnp.concatenate([jnp.full((d, x.shape[1]), fill, x.dtype), x[:t - d]], axis=0)
    rows = lax.broadcasted_iota(jnp.int32, x.shape, 0)
    return jnp.where(rows < d, fill, pltpu.roll(x, d, axis=0))


def _mixers_kernel(rest_ref, lcw_ref, lcb_ref, wg_ref, ba_ref, bx_ref, lam_ref,
                   cw_ref, cb_ref, lng_ref, lnb_ref, pw_ref, pb_ref,
                   y_ref, ltail_ref, h_ref, ctail_ref, lwin_ref, cwin_ref):
    t = MIX_TILE

    @pl.when(pl.program_id(1) == 0)
    def _():
        ltail_ref[...] = jnp.zeros_like(ltail_ref)
        h_ref[...] = jnp.zeros_like(h_ref)
        ctail_ref[...] = jnp.zeros_like(ctail_ref)

    xr = rest_ref[0, :, 0:D_LRU]
    xg = rest_ref[0, :, D_LRU:2 * D_LRU]
    val = rest_ref[0, :, 2 * D_LRU:2 * D_LRU + D_CONF]
    gate = rest_ref[0, :, 2 * D_LRU + D_CONF:]

    lwin_ref[0:LRU_HALO, :] = ltail_ref[...]
    lwin_ref[LRU_HALO:, :] = xr
    ltail_ref[...] = xr[t - LRU_HALO:, :]
    conv = lcb_ref[...]
    for k in range(LRU_CONV_WIDTH):
        off = LRU_HALO - (LRU_CONV_WIDTH - 1) + k
        conv = conv + lcw_ref[k:k + 1, :] * lwin_ref[off:off + t, :]
    gates = jnp.dot(conv.astype(BF16), wg_ref[...], preferred_element_type=F32)
    gate_r = gates[:, 0:D_LRU] + ba_ref[...]
    gate_i = gates[:, D_LRU:] + bx_ref[...]
    log_a = LRU_C * jax.nn.sigmoid(gate_r) * jax.nn.log_sigmoid(lam_ref[...])
    a = jnp.exp(log_a)
    th = jnp.tanh(log_a)
    u = jnp.sqrt(-2.0 * th / (1.0 - th)) * (jax.nn.sigmoid(gate_i) * conv)
    d = 1
    while d < t:
        u = a * _shift_rows(u, d, 0.0) + u
        a = a * _shift_rows(a, d, 1.0)
        d *= 2
    h = a * h_ref[...] + u
    h_ref[...] = h[t - 1:t, :]
    y_ref[0, :, 0:D_LRU] = h * jax.nn.gelu(xg)

    glu = val * jax.nn.sigmoid(gate)
    cwin_ref[0:CONF_HALO, :] = ctail_ref[...]
    cwin_ref[CONF_HALO:, :] = glu
    ctail_ref[...] = glu[t - CONF_HALO:, :]
    acc = cb_ref[...]
    for k in range(CONF_CONV_WIDTH):
        off = CONF_HALO - (CONF_CONV_WIDTH - 1) + k
        acc = acc + cw_ref[k:k + 1, :] * cwin_ref[off:off + t, :]
    mu = jnp.mean(acc, axis=-1, keepdims=True)
    cen = acc - mu
    var = jnp.mean(cen * cen, axis=-1, keepdims=True)
    ln = cen * lax.rsqrt(var + EPS) * lng_ref[...] + lnb_ref[...]
    sw = (ln * jax.nn.sigmoid(ln)).astype(BF16)
    y_ref[0, :, D_LRU:] = jnp.dot(sw, pw_ref[...], preferred_element_type=F32) + pb_ref[...]


def _mixers(rest, bsz, seq, lcw, lcb, wgates, ba, bx, lam, cw, cb, lng, lnb, pw, pb):
    t = MIX_TILE
    rest3 = rest.reshape(bsz, seq, D_REST)
    const = lambda b, i: (0, 0)
    row = lambda d: pl.BlockSpec((1, d), const)
    return pl.pallas_call(
        _mixers_kernel,
        grid=(bsz, seq // t),
        in_specs=[
            pl.BlockSpec((1, t, D_REST), lambda b, i: (b, i, 0)),
            pl.BlockSpec((LRU_CONV_WIDTH, D_LRU), const), row(D_LRU),
            pl.BlockSpec((D_LRU, 2 * D_LRU), const), row(D_LRU), row(D_LRU), row(D_LRU),
            pl.BlockSpec((CONF_CONV_WIDTH, D_CONF), const), row(D_CONF),
            row(D_CONF), row(D_CONF),
            pl.BlockSpec((D_CONF, D_CONF), const), row(D_CONF),
        ],
        out_specs=pl.BlockSpec((1, t, D_LRU + D_CONF), lambda b, i: (b, i, 0)),
        out_shape=jax.ShapeDtypeStruct((bsz, seq, D_LRU + D_CONF), F32),
        scratch_shapes=[
            pltpu.VMEM((LRU_HALO, D_LRU), F32),
            pltpu.VMEM((1, D_LRU), F32),
            pltpu.VMEM((CONF_HALO, D_CONF), F32),
            pltpu.VMEM((LRU_HALO + t, D_LRU), F32),
            pltpu.VMEM((CONF_HALO + t, D_CONF), F32),
        ],
        compiler_params=_cparams("arbitrary", "arbitrary"),
        name="mixers",
    )(rest3, lcw, lcb, wgates, ba, bx, lam, cw, cb, lng, lnb, pw, pb)


ROUTE_G1, ROUTE_G2, ROUTE_E1, ROUTE_E2, ROUTE_R1, ROUTE_R2 = range(6)
ROUTER_FINE_COL0 = N_GROUPS


def _outproj_kernel(ya_ref, ybc_ref, x_ref, gout_ref, wout_ref, g2_ref, wr_ref, ltri_ref,
                    x1_ref, h2_ref, route_ref, cnt_ref):
    @pl.when(pl.program_id(0) == 0)
    def _():
        cnt_ref[...] = jnp.zeros_like(cnt_ref)

    y = jnp.concatenate(
        [_rms(ya_ref[...]), _rms(ybc_ref[:, 0:D_LRU]), _rms(ybc_ref[:, D_LRU:])], axis=1)
    y = (y * gout_ref[...]).astype(BF16)
    x1 = x_ref[...] + jnp.dot(y, wout_ref[...], preferred_element_type=F32)
    x1_ref[...] = x1
    h2 = _rms(x1) * g2_ref[...]
    h2_ref[...] = h2
    lf = jnp.dot(h2.astype(BF16), wr_ref[...], preferred_element_type=F32)

    tm = lf.shape[0]
    col_i = lax.broadcasted_iota(jnp.int32, (tm, LANES), 1)
    col = col_i.astype(F32)
    neg = -jnp.inf
    big = float(LANES)
    is_c = col_i < N_GROUPS
    lc = jnp.where(is_c, lf, neg)
    mc = jnp.max(lc, axis=-1, keepdims=True)
    grp = jnp.min(jnp.where(lc == mc, col, big), axis=-1, keepdims=True)
    w_grp = 1.0 / jnp.sum(jnp.where(is_c, jnp.exp(lf - mc), 0.0), axis=-1, keepdims=True)
    lo = ROUTER_FINE_COL0 + EXPERTS_PER_GROUP * grp
    in_grp = jnp.logical_and(col >= lo, col < lo + EXPERTS_PER_GROUP)
    l1 = jnp.where(in_grp, lf, neg)
    v1 = jnp.max(l1, axis=-1, keepdims=True)
    i1 = jnp.min(jnp.where(l1 == v1, col, big), axis=-1, keepdims=True)
    sel1 = col == i1
    l2 = jnp.where(sel1, neg, l1)
    v2 = jnp.max(l2, axis=-1, keepdims=True)
    i2 = jnp.min(jnp.where(l2 == v2, col, big), axis=-1, keepdims=True)
    sel2 = col == i2
    e21 = jnp.exp(v2 - v1)
    g1 = w_grp / (1.0 + e21)
    g2 = w_grp * e21 / (1.0 + e21)

    oh1 = jnp.where(sel1, 1.0, 0.0)
    oh2 = jnp.where(sel2, 1.0, 0.0)
    before1 = jnp.dot(ltri_ref[...], oh1.astype(BF16), preferred_element_type=F32)
    before2 = jnp.dot(ltri_ref[...], oh2.astype(BF16), preferred_element_type=F32)
    cnt = cnt_ref[...]
    tot1 = jnp.sum(oh1, axis=0, keepdims=True)
    tot2 = jnp.sum(oh2, axis=0, keepdims=True)
    r1 = jnp.sum(jnp.where(sel1, before1 + cnt, 0.0), axis=-1, keepdims=True)
    r2 = jnp.sum(jnp.where(sel2, before2 + cnt + tot1, 0.0), axis=-1, keepdims=True)
    cnt_ref[...] = cnt + tot1 + tot2

    out = jnp.zeros((tm, LANES), F32)
    for lane_id, val in ((ROUTE_G1, g1), (ROUTE_G2, g2),
                         (ROUTE_E1, i1 - ROUTER_FINE_COL0), (ROUTE_E2, i2 - ROUTER_FINE_COL0),
                         (ROUTE_R1, r1), (ROUTE_R2, r2)):
        out = jnp.where(col_i == lane_id, val, out)
    route_ref[...] = out


def _outproj(ya, ybc, x, gout, wout, g2, wr, ltri):
    n = x.shape[0]
    tm = ROW_TILE
    const = lambda i: (0, 0)
    tile = lambda d: pl.BlockSpec((tm, d), lambda i: (i, 0))
    return pl.pallas_call(
        _outproj_kernel,
        grid=(n // tm,),
        in_specs=[
            tile(D_SB), tile(D_LRU + D_CONF), tile(D_MODEL),
            pl.BlockSpec((1, D_MODEL), const),
            pl.BlockSpec((D_MODEL, D_MODEL), const),
            pl.BlockSpec((1, D_MODEL), const),
            pl.BlockSpec((D_MODEL, LANES), const),
            pl.BlockSpec((tm, tm), const),
        ],
        out_specs=[tile(D_MODEL), tile(D_MODEL), tile(LANES)],
        out_shape=[
            jax.ShapeDtypeStruct((n, D_MODEL), F32),
            jax.ShapeDtypeStruct((n, D_MODEL), F32),
            jax.ShapeDtypeStruct((n, LANES), F32),
        ],
        scratch_shapes=[pltpu.VMEM((1, LANES), F32)],
        compiler_params=_cparams("arbitrary"),
        name="outproj",
    )(ya, ybc, x, gout, wout, g2, wr, ltri)


def _row_copy(src_ref, src_row, dst_ref, dst_row, sem):
    return pltpu.make_async_copy(src_ref.at[pl.ds(src_row, 1)], dst_ref.at[pl.ds(dst_row, 1)], sem)


def _dispatch_kernel(pos1_ref, pos2_ref, padlo_ref, padhi_ref, nv_ref, h2_ref, xs_ref, zero_ref, sem, zsem):
    tm = ROW_TILE
    i = pl.program_id(0)
    base = i * tm

    @pl.when(i == 0)
    def _():
        zero_ref[...] = jnp.zeros_like(zero_ref)

        def per_expert(e, _):
            def start(r, c):
                _row_copy(zero_ref, 0, xs_ref, r, zsem).start()
                return c
            lax.fori_loop(padlo_ref[e], padhi_ref[e], start, 0)

            def wait(r, c):
                _row_copy(zero_ref, 0, xs_ref, r, zsem).wait()
                return c
            lax.fori_loop(padlo_ref[e], padhi_ref[e], wait, 0)
            return 0
        lax.fori_loop(0, N_EXPERTS, per_expert, 0)

        def idle_copy(b):
            rows = pl.ds(pl.multiple_of(b * MOE_BLOCK, MOE_BLOCK), MOE_BLOCK)
            return pltpu.make_async_copy(zero_ref, xs_ref.at[rows], zsem)

        def idle_start(b, c):
            idle_copy(b).start()
            return c

        def idle_wait(b, c):
            idle_copy(b).wait()
            return c
        n_blocks = xs_ref.shape[0] // MOE_BLOCK
        lax.fori_loop(nv_ref[0], n_blocks, idle_start, 0)
        lax.fori_loop(nv_ref[0], n_blocks, idle_wait, 0)

    def start(r, c):
        _row_copy(h2_ref, r, xs_ref, pos1_ref[base + r], sem).start()
        _row_copy(h2_ref, r, xs_ref, pos2_ref[base + r], sem).start()
        return c
    lax.fori_loop(0, tm, start, 0)

    def wait(r, c):
        _row_copy(h2_ref, r, xs_ref, pos1_ref[base + r], sem).wait()
        _row_copy(h2_ref, r, xs_ref, pos2_ref[base + r], sem).wait()
        return c
    lax.fori_loop(0, tm, wait, 0)


def _dispatch(pos1, pos2, padlo, padhi, n_valid, h2, n_slots):
    n = h2.shape[0]
    tm = ROW_TILE
    return pl.pallas_call(
        _dispatch_kernel,
        grid_spec=pltpu.PrefetchScalarGridSpec(
            num_scalar_prefetch=5,
            grid=(n // tm,),
            in_specs=[pl.BlockSpec((tm, D_MODEL), lambda i, *_: (i, 0))],
            out_specs=pl.BlockSpec(memory_space=pl.ANY),
            scratch_shapes=[
                pltpu.VMEM((MOE_BLOCK, D_MODEL), F32),
                pltpu.SemaphoreType.DMA(()),
                pltpu.SemaphoreType.DMA(()),
            ],
        ),
        out_shape=jax.ShapeDtypeStruct((n_slots, D_MODEL), F32),
        compiler_params=_cparams("arbitrary"),
        name="dispatch",
    )(pos1, pos2, padlo, padhi, n_valid, h2)


def _experts_kernel(be_ref, nv_ref, xs_ref, wg_ref, wu_ref, wd_ref, ys_ref):
    i = pl.program_id(0)

    @pl.when(i < nv_ref[0])
    def _():
        x = xs_ref[...].astype(BF16)
        g = jnp.dot(x, wg_ref[0], preferred_element_type=F32)
        u = jnp.dot(x, wu_ref[0], preferred_element_type=F32)
        a = (g * jax.nn.sigmoid(g) * u).astype(BF16)
        ys_ref[...] = jnp.dot(a, wd_ref[0], preferred_element_type=F32)

    @pl.when(i >= nv_ref[0])
    def _():
        ys_ref[...] = jnp.zeros_like(ys_ref)


def _experts(block_e, n_valid, xs, wg, wu, wd):
    n_slots = xs.shape[0]
    n_blocks = n_slots // MOE_BLOCK
    return pl.pallas_call(
        _experts_kernel,
        grid_spec=pltpu.PrefetchScalarGridSpec(
            num_scalar_prefetch=2,
            grid=(n_blocks,),
            in_specs=[
                pl.BlockSpec((MOE_BLOCK, D_MODEL), lambda i, be, nv: (jnp.minimum(i, nv[0] - 1), 0)),
                pl.BlockSpec((1, D_MODEL, D_EXPERT), lambda i, be, nv: (be[i], 0, 0)),
                pl.BlockSpec((1, D_MODEL, D_EXPERT), lambda i, be, nv: (be[i], 0, 0)),
                pl.BlockSpec((1, D_EXPERT, D_MODEL), lambda i, be, nv: (be[i], 0, 0)),
            ],
            out_specs=pl.BlockSpec((MOE_BLOCK, D_MODEL), lambda i, be, nv: (i, 0)),
        ),
        out_shape=jax.ShapeDtypeStruct((n_slots, D_MODEL), F32),
        compiler_params=_cparams("arbitrary"),
        name="experts",
    )(block_e, n_valid, xs, wg, wu, wd)


def _combine_kernel(pos1_ref, pos2_ref, ys_ref, x1_ref, route_ref, out_ref, buf_ref, sems):
    tm = ROW_TILE
    i = pl.program_id(0)
    n_steps = pl.num_programs(0)

    def gather(step, slot, action):
        base = step * tm

        def body(r, c):
            for which, pos_ref in enumerate((pos1_ref, pos2_ref)):
                cp = pltpu.make_async_copy(
                    ys_ref.at[pl.ds(pos_ref[base + r], 1)],
                    buf_ref.at[slot, which, pl.ds(r, 1)],
                    sems.at[slot])
                if action == "start":
                    cp.start()
                else:
                    cp.wait()
            return c
        lax.fori_loop(0, tm, body, 0)

    slot = lax.rem(i, 2)

    @pl.when(i == 0)
    def _():
        gather(0, 0, "start")

    @pl.when(i + 1 < n_steps)
    def _():
        gather(i + 1, 1 - slot, "start")

    gather(i, slot, "wait")
    col = lax.broadcasted_iota(jnp.int32, (tm, LANES), 1)
    route = route_ref[...]
    g1 = jnp.sum(jnp.where(col == ROUTE_G1, route, 0.0), axis=-1, keepdims=True)
    g2 = jnp.sum(jnp.where(col == ROUTE_G2, route, 0.0), axis=-1, keepdims=True)
    out_ref[...] = x1_ref[...] + (buf_ref[slot, 0] * g1 + buf_ref[slot, 1] * g2)


def _combine(pos1, pos2, ys, x1, route):
    n = x1.shape[0]
    tm = ROW_TILE
    return pl.pallas_call(
        _combine_kernel,
        grid_spec=pltpu.PrefetchScalarGridSpec(
            num_scalar_prefetch=2,
            grid=(n // tm,),
            in_specs=[
                pl.BlockSpec(memory_space=pl.ANY),
                pl.BlockSpec((tm, D_MODEL), lambda i, *_: (i, 0)),
                pl.BlockSpec((tm, LANES), lambda i, *_: (i, 0)),
            ],
            out_specs=pl.BlockSpec((tm, D_MODEL), lambda i, *_: (i, 0)),
            scratch_shapes=[
                pltpu.VMEM((2, 2, tm, D_MODEL), F32),
                pltpu.SemaphoreType.DMA((2,)),
            ],
        ),
        out_shape=jax.ShapeDtypeStruct((n, D_MODEL), F32),
        compiler_params=_cparams("arbitrary"),
        name="combine",
    )(pos1, pos2, ys, x1, route)


def _slot_plan(route, n_tok):
    e1 = route[:, ROUTE_E1].astype(jnp.int32)
    e2 = route[:, ROUTE_E2].astype(jnp.int32)
    r1 = route[:, ROUTE_R1].astype(jnp.int32)
    r2 = route[:, ROUTE_R2].astype(jnp.int32)
    experts = jnp.arange(N_EXPERTS, dtype=jnp.int32)
    counts = (jnp.sum((e1[:, None] == experts).astype(jnp.int32), axis=0)
              + jnp.sum((e2[:, None] == experts).astype(jnp.int32), axis=0))
    padded = (counts + MOE_BLOCK - 1) // MOE_BLOCK * MOE_BLOCK
    padded_ends = jnp.cumsum(padded)
    padded_starts = padded_ends - padded
    pos1 = padded_starts[e1] + r1
    pos2 = padded_starts[e2] + r2
    n_blocks = -(-(2 * n_tok) // MOE_BLOCK) + N_EXPERTS
    n_valid = (padded_ends[-1] // MOE_BLOCK).astype(jnp.int32)
    block_start = jnp.arange(n_blocks, dtype=jnp.int32) * MOE_BLOCK
    block_start = jnp.minimum(block_start, padded_ends[-1] - MOE_BLOCK)
    block_e = jnp.minimum(jnp.searchsorted(padded_ends, block_start, side='right'),
                          N_EXPERTS - 1).astype(jnp.int32)
    return dict(pos1=pos1, pos2=pos2, padlo=(padded_starts + counts).astype(jnp.int32),
                padhi=padded_ends.astype(jnp.int32), block_e=block_e,
                n_valid=n_valid.reshape(1), n_slots=n_blocks * MOE_BLOCK)


def _block_diag(w):
    nb, r, c = w.shape
    eye = jnp.eye(nb, dtype=w.dtype)
    return (w[:, :, None, :] * eye[:, None, :, None]).reshape(nb * r, nb * c)


def kernel(x, norm1_g, w_in, q_norm_g, k_norm_g, lru_conv_w, lru_conv_b, lru_wa, lru_ba, lru_wx, lru_bx, lru_lambda, conf_dw_w, conf_dw_b, conf_ln_g, conf_ln_b, conf_pw_w, conf_pw_b, out_norm_g, w_out, norm2_g, router_coarse, router_fine, exp_w_gate, exp_w_up, exp_w_down):
    bsz, seq, d = x.shape
    n = bsz * seq
    depth = w_in.shape[0]
    assert d == D_MODEL and n % ROW_TILE == 0 and seq % MIX_TILE == 0 and seq % ATTN_BLOCK == 0

    heads = D_SB // HEAD_DIM
    head_of = jnp.arange(D_SB, dtype=jnp.int32) // HEAD_DIM
    gsum = (head_of[:, None] == head_of[None, :]).astype(BF16)
    t_io = jnp.arange(ROW_TILE, dtype=jnp.int32)
    ltri = (t_io[None, :] < t_io[:, None]).astype(BF16)
    row = lambda v: v.reshape(1, -1)

    xf = x.reshape(n, d)
    for l in range(depth):
        qkv, rest = _inproj(xf, row(norm1_g[l]), w_in[l].astype(BF16),
                            row(jnp.tile(q_norm_g[l], heads)), row(jnp.tile(k_norm_g[l], heads)), gsum)
        ya = _attention(qkv, bsz, seq).reshape(n, D_SB)
        wgates = jnp.concatenate([_block_diag(lru_wa[l]), _block_diag(lru_wx[l])], axis=1).astype(BF16)
        ybc = _mixers(rest, bsz, seq, lru_conv_w[l], row(lru_conv_b[l]), wgates,
                      row(lru_ba[l]), row(lru_bx[l]), row(lru_lambda[l]),
                      conf_dw_w[l], row(conf_dw_b[l]), row(conf_ln_g[l]), row(conf_ln_b[l]),
                      conf_pw_w[l].astype(BF16), row(conf_pw_b[l])).reshape(n, D_LRU + D_CONF)
        wr = jnp.concatenate([router_coarse[l], router_fine[l]], axis=1)
        wr = jnp.pad(wr, ((0, 0), (0, LANES - wr.shape[1]))).astype(BF16)
        x1, h2, route = _outproj(ya, ybc, xf, row(out_norm_g[l]), w_out[l].astype(BF16),
                                 row(norm2_g[l]), wr, ltri)
        plan = _slot_plan(route, n)
        xs = _dispatch(plan["pos1"], plan["pos2"], plan["padlo"], plan["padhi"], plan["n_valid"], h2,
                       plan["n_slots"])
        ys = _experts(plan["block_e"], plan["n_valid"], xs, exp_w_gate[l].astype(BF16),
                      exp_w_up[l].astype(BF16), exp_w_down[l].astype(BF16))
        xf = _combine(plan["pos1"], plan["pos2"], ys, x1, route)
    return xf.reshape(bsz, seq, d)
```

```python
import functools

import jax
import jax.numpy as jnp
from jax import lax
from jax.experimental import pallas as pl
from jax.experimental.pallas import tpu as pltpu

F32 = jnp.float32
BF16 = jnp.bfloat16

D_MODEL = 1024
HEAD_DIM = 64
D_SB = 512
D_LRU = 256
LRU_BLOCKS = 4
LRU_CONV_WIDTH = 4
LRU_C = 8.0
D_CONF = 256
CONF_CONV_WIDTH = 31
D_QKV = 3 * D_SB
D_REST = 2 * D_LRU + 2 * D_CONF
N_GROUPS = 4
EXPERTS_PER_GROUP = 8
N_EXPERTS = N_GROUPS * EXPERTS_PER_GROUP
D_EXPERT = 512
MOE_BLOCK = 256
EPS = 1e-6

LANES = 128
ROW_TILE = 256
ATTN_BQ = 256
ATTN_BK = 128
MIX_TILE = 512
CONF_HALO = 32
DMA_UNROLL = 8
LRU_HALO = 8
ATTN_DEAD_LOG = -110.0
VMEM_LIMIT = 48 * 1024 * 1024


def _cparams(*sem):
    return pltpu.CompilerParams(dimension_semantics=sem, vmem_limit_bytes=VMEM_LIMIT)


def _rms(y):
    return y * lax.rsqrt(jnp.mean(y * y, axis=-1, keepdims=True) + EPS)


def _inproj_kernel(x_ref, g_ref, w_ref, qg_ref, kg_ref, gsum_ref, qkv_ref, rest_ref):
    h = (_rms(x_ref[...]) * g_ref[...]).astype(BF16)

    def head_norm(p, gain):
        s = jnp.dot((p * p).astype(BF16), gsum_ref[...], preferred_element_type=F32)
        return p * lax.rsqrt(s * (1.0 / HEAD_DIM) + EPS) * gain

    q = jnp.dot(h, w_ref[:, 0:D_SB], preferred_element_type=F32)
    qkv_ref[:, 0:D_SB] = head_norm(q, qg_ref[...]).astype(BF16)
    k = jnp.dot(h, w_ref[:, D_SB:2 * D_SB], preferred_element_type=F32)
    qkv_ref[:, D_SB:2 * D_SB] = head_norm(k, kg_ref[...]).astype(BF16)
    v = jnp.dot(h, w_ref[:, 2 * D_SB:D_QKV], preferred_element_type=F32)
    qkv_ref[:, 2 * D_SB:D_QKV] = v.astype(BF16)
    rest_ref[...] = jnp.dot(h, w_ref[:, D_QKV:], preferred_element_type=F32)


def _inproj(x, g, w, qg, kg, gsum):
    n = x.shape[0]
    tm = ROW_TILE
    const = lambda i: (0, 0)
    return pl.pallas_call(
        _inproj_kernel,
        grid=(n // tm,),
        in_specs=[
            pl.BlockSpec((tm, D_MODEL), lambda i: (i, 0)),
            pl.BlockSpec((1, D_MODEL), const),
            pl.BlockSpec((D_MODEL, D_QKV + D_REST), const),
            pl.BlockSpec((1, D_SB), const),
            pl.BlockSpec((1, D_SB), const),
            pl.BlockSpec((D_SB, D_SB), const),
        ],
        out_specs=[
            pl.BlockSpec((tm, D_QKV), lambda i: (i, 0)),
            pl.BlockSpec((tm, D_REST), lambda i: (i, 0)),
        ],
        out_shape=[
            jax.ShapeDtypeStruct((n, D_QKV), BF16),
            jax.ShapeDtypeStruct((n, D_REST), F32),
        ],
        compiler_params=_cparams("arbitrary"),
        name="inproj",
    )(x, g, w, qg, kg, gsum)


def _attn_kernel(q_ref, k_ref, v_ref, o_ref, z0, z1, lbc0, lbc1, hi0, hi1, lo0, lo1, c0, c1, acc_ref):
    bq, bk = ATTN_BQ, ATTN_BK
    heads = (0, 1)
    z_refs, lbc_refs, hi_refs, lo_refs, c_refs = (z0, z1), (lbc0, lbc1), (hi0, hi1), (lo0, lo1), (c0, c1)
    i = pl.program_id(2)
    q = q_ref[0]
    first = lax.broadcasted_iota(jnp.int32, (bq, LANES), 1) < HEAD_DIM
    scale = jnp.asarray(HEAD_DIM ** -0.5, BF16)
    zero = jnp.zeros_like(q)
    q_heads = (jnp.where(first, q, zero) * scale, jnp.where(first, zero, q) * scale)
    t_pos = lax.broadcasted_iota(jnp.int32, (bq, bk), 0) + i * bq
    s_io = lax.broadcasted_iota(jnp.int32, (bq, bk), 1)
    later = (lax.broadcasted_iota(jnp.int32, (bk, bk), 0)
             > lax.broadcasted_iota(jnp.int32, (bk, bk), 1)).astype(BF16)

    def rows(m):
        return pl.ds(pl.multiple_of(jnp.maximum(m, 0) * bk, bk), bk)

    def causal(m):
        return (s_io + m * bk) < t_pos

    def stage0(m, s):
        kb = k_ref[0, rows(m), :]
        for h in heads:
            z_refs[s][h] = lax.dot_general(q_heads[h], kb, (((1,), (1,)), ((), ())),
                                           preferred_element_type=F32)

    def stage1(m, s, masked):
        mask = causal(m) if masked else None
        for h in heads:
            z = z_refs[s][h]
            c = c_refs[s][h]
            soft = jnp.log(1.0 + jnp.exp(-jnp.abs(z)))
            log_keep = -(jnp.maximum(z, 0.0) + soft)
            lbc_refs[s][h] = (z + log_keep) + c
            if masked:
                log_keep = jnp.where(mask, log_keep, 0.0)
            hi = log_keep.astype(BF16)
            hi_refs[s][h] = hi
            lo_refs[s][h] = (log_keep - hi.astype(F32)).astype(BF16)
            c_refs[1 - s][h] = c + jnp.sum(log_keep, axis=-1, keepdims=True)

    def stage2_sums(s):
        return [jnp.dot(hi_refs[s][h], later, preferred_element_type=F32)
                + jnp.dot(lo_refs[s][h], later, preferred_element_type=F32) for h in heads]

    def stage2_apply(m, s, within, masked):
        vb = v_ref[0, rows(m), :]
        mask = causal(m) if masked else None
        for h in heads:
            w = jnp.exp(lbc_refs[s][h] + within[h])
            if masked:
                w = jnp.where(mask, w, 0.0)
            acc_ref[h] += jnp.dot(w.astype(BF16), vb, preferred_element_type=F32)

    def alive(m, s):
        return jnp.logical_and(m >= 0, jnp.max(c_refs[s][...]) > ATTN_DEAD_LOG)

    top = (i + 1) * (bq // bk) - 1
    acc_ref[...] = jnp.zeros_like(acc_ref)
    c_refs[1][...] = jnp.zeros((2, bq, LANES), F32)
    def step(m, s, masked_next=False, masked=False):
        live_next = alive(m - 1, 1 - s)
        within = stage2_sums(s)
        stage0(m - 2, s)
        stage1(m - 1, 1 - s, masked_next)
        stage2_apply(m, s, within, masked)
        return live_next

    stage0(top, 1)
    stage0(top - 1, 0)
    stage1(top, 1, True)
    step(top, 1, masked_next=True, masked=True)
    step(top - 1, 0, masked=True)

    def body(st):
        m, _ = st
        live_next = lax.cond((m & 1) == 0, functools.partial(step, m, 0), functools.partial(step, m, 1))
        return m - 1, live_next

    lax.while_loop(lambda st: st[1], body, (top - 2, alive(top - 2, 1)))
    o_ref[0] = jnp.where(first, acc_ref[0], acc_ref[1])


def _attention(qkv, bsz, seq):
    bq, bk = ATTN_BQ, ATTN_BK
    assert bq == 2 * bk
    pairs = D_SB // LANES
    qkv3 = qkv.reshape(bsz, seq, D_QKV)
    return pl.pallas_call(
        _attn_kernel,
        grid=(bsz, pairs, seq // bq),
        in_specs=[
            pl.BlockSpec((1, bq, LANES), lambda b, p, i: (b, i, p)),
            pl.BlockSpec((1, seq, LANES), lambda b, p, i: (b, 0, pairs + p)),
            pl.BlockSpec((1, seq, LANES), lambda b, p, i: (b, 0, 2 * pairs + p)),
        ],
        out_specs=pl.BlockSpec((1, bq, LANES), lambda b, p, i: (b, i, p)),
        out_shape=jax.ShapeDtypeStruct((bsz, seq, D_SB), F32),
        scratch_shapes=(
            [pltpu.VMEM((2, bq, bk), F32)] * 2
            + [pltpu.VMEM((2, bq, bk), F32)] * 2
            + [pltpu.VMEM((2, bq, bk), BF16)] * 2
            + [pltpu.VMEM((2, bq, bk), BF16)] * 2
            + [pltpu.VMEM((2, bq, LANES), F32)] * 2
            + [pltpu.VMEM((2, bq, LANES), F32)]
        ),
        compiler_params=_cparams("arbitrary", "arbitrary", "arbitrary"),
        name="attn",
    )(qkv3, qkv3, qkv3)


def _shift_rows(x, d, fill):
    t = x.shape[0]
    if d % 8 == 0:
        return jnp.concatenate([jnp.full((d, x.shape[1]), fill, x.dtype), x[:t - d]], axis=0)
    rows = lax.broadcasted_iota(jnp.int32, x.shape, 0)
    return jnp.where(rows < d, fill, pltpu.roll(x, d, axis=0))


def _mixers_kernel(rest_ref, lcw_ref, lcb_ref, wg_ref, ba_ref, bx_ref, lam_ref,
                   cw_ref, cb_ref, lng_ref, lnb_ref, pw_ref, pb_ref,
                   y_ref, ltail_ref, h_ref, ctail_ref, lwin_ref, cwin_ref):
    t = MIX_TILE

    @pl.when(pl.program_id(1) == 0)
    def _():
        ltail_ref[...] = jnp.zeros_like(ltail_ref)
        h_ref[...] = jnp.zeros_like(h_ref)
        ctail_ref[...] = jnp.zeros_like(ctail_ref)

    xr = rest_ref[0, :, 0:D_LRU]
    xg = rest_ref[0, :, D_LRU:2 * D_LRU]
    val = rest_ref[0, :, 2 * D_LRU:2 * D_LRU + D_CONF]
    gate = rest_ref[0, :, 2 * D_LRU + D_CONF:]

    lwin_ref[0:LRU_HALO, :] = ltail_ref[...]
    lwin_ref[LRU_HALO:, :] = xr
    ltail_ref[...] = xr[t - LRU_HALO:, :]
    conv = lcb_ref[...]
    for k in range(LRU_CONV_WIDTH):
        off = LRU_HALO - (LRU_CONV_WIDTH - 1) + k
        conv = conv + lcw_ref[k:k + 1, :] * lwin_ref[off:off + t, :]
    gates = jnp.dot(conv.astype(BF16), wg_ref[...], preferred_element_type=F32)
    gate_r = gates[:, 0:D_LRU] + ba_ref[...]
    gate_i = gates[:, D_LRU:] + bx_ref[...]
    log_a = LRU_C * jax.nn.sigmoid(gate_r) * jax.nn.log_sigmoid(lam_ref[...])
    a = jnp.exp(log_a)
    th = jnp.tanh(log_a)
    u = jnp.sqrt(-2.0 * th / (1.0 - th)) * (jax.nn.sigmoid(gate_i) * conv)
    d = 1
    while d < t:
        u = a * _shift_rows(u, d, 0.0) + u
        a = a * _shift_rows(a, d, 1.0)
        d *= 2
    h = a * h_ref[...] + u
    h_ref[...] = h[t - 1:t, :]
    y_ref[0, :, 0:D_LRU] = h * jax.nn.gelu(xg)

    glu = val * jax.nn.sigmoid(gate)
    cwin_ref[0:CONF_HALO, :] = ctail_ref[...]
    cwin_ref[CONF_HALO:, :] = glu
    ctail_ref[...] = glu[t - CONF_HALO:, :]
    acc = cb_ref[...]
    for k in range(CONF_CONV_WIDTH):
        off = CONF_HALO - (CONF_CONV_WIDTH - 1) + k
        acc = acc + cw_ref[k:k + 1, :] * cwin_ref[off:off + t, :]
    mu = jnp.mean(acc, axis=-1, keepdims=True)
    cen = acc - mu
    var = jnp.mean(cen * cen, axis=-1, keepdims=True)
    ln = cen * lax.rsqrt(var + EPS) * lng_ref[...] + lnb_ref[...]
    sw = (ln * jax.nn.sigmoid(ln)).astype(BF16)
    y_ref[0, :, D_LRU:] = jnp.dot(sw, pw_ref[...], preferred_element_type=F32) + pb_ref[...]


def _mixers(rest, bsz, seq, lcw, lcb, wgates, ba, bx, lam, cw, cb, lng, lnb, pw, pb):
    t = MIX_TILE
    rest3 = rest.reshape(bsz, seq, D_REST)
    const = lambda b, i: (0, 0)
    row = lambda d: pl.BlockSpec((1, d), const)
    return pl.pallas_call(
        _mixers_kernel,
        grid=(bsz, seq // t),
        in_specs=[
            pl.BlockSpec((1, t, D_REST), lambda b, i: (b, i, 0)),
            pl.BlockSpec((LRU_CONV_WIDTH, D_LRU), const), row(D_LRU),
            pl.BlockSpec((D_LRU, 2 * D_LRU), const), row(D_LRU), row(D_LRU), row(D_LRU),
            pl.BlockSpec((CONF_CONV_WIDTH, D_CONF), const), row(D_CONF),
            row(D_CONF), row(D_CONF),
            pl.BlockSpec((D_CONF, D_CONF), const), row(D_CONF),
        ],
        out_specs=pl.BlockSpec((1, t, D_LRU + D_CONF), lambda b, i: (b, i, 0)),
        out_shape=jax.ShapeDtypeStruct((bsz, seq, D_LRU + D_CONF), F32),
        scratch_shapes=[
            pltpu.VMEM((LRU_HALO, D_LRU), F32),
            pltpu.VMEM((1, D_LRU), F32),
            pltpu.VMEM((CONF_HALO, D_CONF), F32),
            pltpu.VMEM((LRU_HALO + t, D_LRU), F32),
            pltpu.VMEM((CONF_HALO + t, D_CONF), F32),
        ],
        compiler_params=_cparams("arbitrary", "arbitrary"),
        name="mixers",
    )(rest3, lcw, lcb, wgates, ba, bx, lam, cw, cb, lng, lnb, pw, pb)


ROUTE_G1, ROUTE_G2, ROUTE_E1, ROUTE_E2, ROUTE_R1, ROUTE_R2 = range(6)
ROUTE_ROWS = 8
ROUTER_FINE_COL0 = N_GROUPS


def _outproj_kernel(ya_ref, ybc_ref, x_ref, gout_ref, wout_ref, g2_ref, wr_ref, ltri_ref,
                    x1_ref, h2_ref, route_ref, route_t_ref, counts_ref, cnt_ref):
    @pl.when(pl.program_id(0) == 0)
    def _():
        cnt_ref[...] = jnp.zeros_like(cnt_ref)

    y = jnp.concatenate(
        [_rms(ya_ref[...]), _rms(ybc_ref[:, 0:D_LRU]), _rms(ybc_ref[:, D_LRU:])], axis=1)
    y = (y * gout_ref[...]).astype(BF16)
    x1 = x_ref[...] + jnp.dot(y, wout_ref[...], preferred_element_type=F32)
    x1_ref[...] = x1
    h2 = _rms(x1) * g2_ref[...]
    h2_ref[...] = h2
    lf = jnp.dot(h2.astype(BF16), wr_ref[...], preferred_element_type=F32)

    tm = lf.shape[0]
    col_i = lax.broadcasted_iota(jnp.int32, (tm, LANES), 1)
    col = col_i.astype(F32)
    neg = -jnp.inf
    big = float(LANES)
    is_c = col_i < N_GROUPS
    lc = jnp.where(is_c, lf, neg)
    mc = jnp.max(lc, axis=-1, keepdims=True)
    grp = jnp.min(jnp.where(lc == mc, col, big), axis=-1, keepdims=True)
    w_grp = 1.0 / jnp.sum(jnp.where(is_c, jnp.exp(lf - mc), 0.0), axis=-1, keepdims=True)
    lo = ROUTER_FINE_COL0 + EXPERTS_PER_GROUP * grp
    in_grp = jnp.logical_and(col >= lo, col < lo + EXPERTS_PER_GROUP)
    l1 = jnp.where(in_grp, lf, neg)
    v1 = jnp.max(l1, axis=-1, keepdims=True)
    i1 = jnp.min(jnp.where(l1 == v1, col, big), axis=-1, keepdims=True)
    sel1 = col == i1
    l2 = jnp.where(sel1, neg, l1)
    v2 = jnp.max(l2, axis=-1, keepdims=True)
    i2 = jnp.min(jnp.where(l2 == v2, col, big), axis=-1, keepdims=True)
    sel2 = col == i2
    e21 = jnp.exp(v2 - v1)
    g1 = w_grp / (1.0 + e21)
    g2 = w_grp * e21 / (1.0 + e21)

    oh1 = jnp.where(sel1, 1.0, 0.0)
    oh2 = jnp.where(sel2, 1.0, 0.0)
    before1 = jnp.dot(ltri_ref[...], oh1.astype(BF16), preferred_element_type=F32)
    before2 = jnp.dot(ltri_ref[...], oh2.astype(BF16), preferred_element_type=F32)
    cnt = cnt_ref[...]
    tot1 = jnp.sum(oh1, axis=0, keepdims=True)
    tot2 = jnp.sum(oh2, axis=0, keepdims=True)
    r1 = jnp.sum(jnp.where(sel1, before1 + cnt, 0.0), axis=-1, keepdims=True)
    r2 = jnp.sum(jnp.where(sel2, before2 + cnt + tot1, 0.0), axis=-1, keepdims=True)
    cnt_ref[...] = cnt + tot1 + tot2

    out = jnp.zeros((tm, LANES), F32)
    for lane_id, val in ((ROUTE_G1, g1), (ROUTE_G2, g2),
                         (ROUTE_E1, i1 - ROUTER_FINE_COL0), (ROUTE_E2, i2 - ROUTER_FINE_COL0),
                         (ROUTE_R1, r1), (ROUTE_R2, r2)):
        out = jnp.where(col_i == lane_id, val, out)
    route_ref[...] = out
    route_t_ref[...] = out.T[0:ROUTE_ROWS, :]
    counts_ref[...] = cnt_ref[...]


def _outproj(ya, ybc, x, gout, wout, g2, wr, ltri):
    n = x.shape[0]
    tm = ROW_TILE
    const = lambda i: (0, 0)
    tile = lambda d: pl.BlockSpec((tm, d), lambda i: (i, 0))
    return pl.pallas_call(
        _outproj_kernel,
        grid=(n // tm,),
        in_specs=[
            tile(D_SB), tile(D_LRU + D_CONF), tile(D_MODEL),
            pl.BlockSpec((1, D_MODEL), const),
            pl.BlockSpec((D_MODEL, D_MODEL), const),
            pl.BlockSpec((1, D_MODEL), const),
            pl.BlockSpec((D_MODEL, LANES), const),
            pl.BlockSpec((tm, tm), const),
        ],
        out_specs=[tile(D_MODEL), tile(D_MODEL), tile(LANES),
                   pl.BlockSpec((ROUTE_ROWS, tm), lambda i: (0, i)),
                   pl.BlockSpec((1, LANES), const)],
        out_shape=[
            jax.ShapeDtypeStruct((n, D_MODEL), F32),
            jax.ShapeDtypeStruct((n, D_MODEL), F32),
            jax.ShapeDtypeStruct((n, LANES), F32),
            jax.ShapeDtypeStruct((ROUTE_ROWS, n), F32),
            jax.ShapeDtypeStruct((1, LANES), F32),
        ],
        scratch_shapes=[pltpu.VMEM((1, LANES), F32)],
        compiler_params=_cparams("arbitrary"),
        name="outproj",
    )(ya, ybc, x, gout, wout, g2, wr, ltri)


def _row_copy(src_ref, src_row, dst_ref, dst_row, sem):
    return pltpu.make_async_copy(src_ref.at[pl.ds(src_row, 1)], dst_ref.at[pl.ds(dst_row, 1)], sem)


def _dispatch_kernel(pos1_ref, pos2_ref, padlo_ref, padhi_ref, nv_ref, h2_ref, xs_ref, zero_ref, sem, zsem):
    tm = ROW_TILE
    i = pl.program_id(0)
    base = i * tm

    @pl.when(i == 0)
    def _():
        zero_ref[...] = jnp.zeros_like(zero_ref)

        def per_expert(e, _):
            def start(r, c):
                _row_copy(zero_ref, 0, xs_ref, r, zsem).start()
                return c
            lax.fori_loop(padlo_ref[e], padhi_ref[e], start, 0)

            def wait(r, c):
                _row_copy(zero_ref, 0, xs_ref, r, zsem).wait()
                return c
            lax.fori_loop(padlo_ref[e], padhi_ref[e], wait, 0)
            return 0
        lax.fori_loop(0, N_EXPERTS, per_expert, 0)

        def idle_copy(b):
            rows = pl.ds(pl.multiple_of(b * MOE_BLOCK, MOE_BLOCK), MOE_BLOCK)
            return pltpu.make_async_copy(zero_ref, xs_ref.at[rows], zsem)

        def idle_start(b, c):
            idle_copy(b).start()
            return c

        def idle_wait(b, c):
            idle_copy(b).wait()
            return c
        n_blocks = xs_ref.shape[0] // MOE_BLOCK
        lax.fori_loop(nv_ref[0], n_blocks, idle_start, 0)
        lax.fori_loop(nv_ref[0], n_blocks, idle_wait, 0)

    def start(r, c):
        _row_copy(h2_ref, r, xs_ref, pos1_ref[base + r], sem).start()
        _row_copy(h2_ref, r, xs_ref, pos2_ref[base + r], sem).start()
        return c
    lax.fori_loop(0, tm, start, 0, unroll=DMA_UNROLL)

    def wait(r, c):
        _row_copy(h2_ref, r, xs_ref, pos1_ref[base + r], sem).wait()
        _row_copy(h2_ref, r, xs_ref, pos2_ref[base + r], sem).wait()
        return c
    lax.fori_loop(0, tm, wait, 0, unroll=DMA_UNROLL)


def _dispatch(pos1, pos2, padlo, padhi, n_valid, h2, n_slots):
    n = h2.shape[0]
    tm = ROW_TILE
    return pl.pallas_call(
        _dispatch_kernel,
        grid_spec=pltpu.PrefetchScalarGridSpec(
            num_scalar_prefetch=5,
            grid=(n // tm,),
            in_specs=[pl.BlockSpec((tm, D_MODEL), lambda i, *_: (i, 0))],
            out_specs=pl.BlockSpec(memory_space=pl.ANY),
            scratch_shapes=[
                pltpu.VMEM((MOE_BLOCK, D_MODEL), F32),
                pltpu.SemaphoreType.DMA(()),
                pltpu.SemaphoreType.DMA(()),
            ],
        ),
        out_shape=jax.ShapeDtypeStruct((n_slots, D_MODEL), F32),
        compiler_params=_cparams("arbitrary"),
        name="dispatch",
    )(pos1, pos2, padlo, padhi, n_valid, h2)


def _experts_kernel(be_ref, nv_ref, xs_ref, wg_ref, wu_ref, wd_ref, ys_ref):
    i = pl.program_id(0)

    @pl.when(i < nv_ref[0])
    def _():
        x = xs_ref[...].astype(BF16)
        g = jnp.dot(x, wg_ref[0], preferred_element_type=F32)
        u = jnp.dot(x, wu_ref[0], preferred_element_type=F32)
        a = (g * jax.nn.sigmoid(g) * u).astype(BF16)
        ys_ref[...] = jnp.dot(a, wd_ref[0], preferred_element_type=F32)

    @pl.when(i >= nv_ref[0])
    def _():
        ys_ref[...] = jnp.zeros_like(ys_ref)


def _experts(block_e, n_valid, xs, wg, wu, wd):
    n_slots = xs.shape[0]
    n_blocks = n_slots // MOE_BLOCK
    return pl.pallas_call(
        _experts_kernel,
        grid_spec=pltpu.PrefetchScalarGridSpec(
            num_scalar_prefetch=2,
            grid=(n_blocks,),
            in_specs=[
                pl.BlockSpec((MOE_BLOCK, D_MODEL), lambda i, be, nv: (jnp.minimum(i, nv[0] - 1), 0)),
                pl.BlockSpec((1, D_MODEL, D_EXPERT), lambda i, be, nv: (be[i], 0, 0)),
                pl.BlockSpec((1, D_MODEL, D_EXPERT), lambda i, be, nv: (be[i], 0, 0)),
                pl.BlockSpec((1, D_EXPERT, D_MODEL), lambda i, be, nv: (be[i], 0, 0)),
            ],
            out_specs=pl.BlockSpec((MOE_BLOCK, D_MODEL), lambda i, be, nv: (i, 0)),
        ),
        out_shape=jax.ShapeDtypeStruct((n_slots, D_MODEL), F32),
        compiler_params=_cparams("arbitrary"),
        name="experts",
    )(block_e, n_valid, xs, wg, wu, wd)


def _combine_kernel(pos1_ref, pos2_ref, ys_ref, x1_ref, route_ref, out_ref, buf_ref, sems):
    tm = ROW_TILE
    i = pl.program_id(0)
    n_steps = pl.num_programs(0)

    def gather(step, slot, action):
        base = step * tm

        def body(r, c):
            for which, pos_ref in enumerate((pos1_ref, pos2_ref)):
                cp = pltpu.make_async_copy(
                    ys_ref.at[pl.ds(pos_ref[base + r], 1)],
                    buf_ref.at[slot, which, pl.ds(r, 1)],
                    sems.at[slot])
                if action == "start":
                    cp.start()
                else:
                    cp.wait()
            return c
        lax.fori_loop(0, tm, body, 0, unroll=DMA_UNROLL)

    slot = lax.rem(i, 2)

    @pl.when(i == 0)
    def _():
        gather(0, 0, "start")

    @pl.when(i + 1 < n_steps)
    def _():
        gather(i + 1, 1 - slot, "start")

    gather(i, slot, "wait")
    col = lax.broadcasted_iota(jnp.int32, (tm, LANES), 1)
    route = route_ref[...]
    g1 = jnp.sum(jnp.where(col == ROUTE_G1, route, 0.0), axis=-1, keepdims=True)
    g2 = jnp.sum(jnp.where(col == ROUTE_G2, route, 0.0), axis=-1, keepdims=True)
    out_ref[...] = x1_ref[...] + (buf_ref[slot, 0] * g1 + buf_ref[slot, 1] * g2)


def _combine(pos1, pos2, ys, x1, route):
    n = x1.shape[0]
    tm = ROW_TILE
    return pl.pallas_call(
        _combine_kernel,
        grid_spec=pltpu.PrefetchScalarGridSpec(
            num_scalar_prefetch=2,
            grid=(n // tm,),
            in_specs=[
                pl.BlockSpec(memory_space=pl.ANY),
                pl.BlockSpec((tm, D_MODEL), lambda i, *_: (i, 0)),
                pl.BlockSpec((tm, LANES), lambda i, *_: (i, 0)),
            ],
            out_specs=pl.BlockSpec((tm, D_MODEL), lambda i, *_: (i, 0)),
            scratch_shapes=[
                pltpu.VMEM((2, 2, tm, D_MODEL), F32),
                pltpu.SemaphoreType.DMA((2,)),
            ],
        ),
        out_shape=jax.ShapeDtypeStruct((n, D_MODEL), F32),
        compiler_params=_cparams("arbitrary"),
        name="combine",
    )(pos1, pos2, ys, x1, route)


def _slot_plan(route_t, counts_row, n_tok):
    e1 = route_t[ROUTE_E1].astype(jnp.int32)
    e2 = route_t[ROUTE_E2].astype(jnp.int32)
    r1 = route_t[ROUTE_R1].astype(jnp.int32)
    r2 = route_t[ROUTE_R2].astype(jnp.int32)
    experts = jnp.arange(N_EXPERTS, dtype=jnp.int32)
    counts = counts_row[0, ROUTER_FINE_COL0:ROUTER_FINE_COL0 + N_EXPERTS].astype(jnp.int32)
    padded = (counts + MOE_BLOCK - 1) // MOE_BLOCK * MOE_BLOCK
    padded_ends = jnp.cumsum(padded)
    padded_starts = padded_ends - padded

    def start_of(e):
        return jnp.sum(jnp.where(e[None, :] == experts[:, None], padded_starts[:, None], 0), axis=0)

    pos1 = start_of(e1) + r1
    pos2 = start_of(e2) + r2
    n_blocks = -(-(2 * n_tok) // MOE_BLOCK) + N_EXPERTS
    n_valid = (padded_ends[-1] // MOE_BLOCK).astype(jnp.int32)
    block_start = jnp.arange(n_blocks, dtype=jnp.int32) * MOE_BLOCK
    block_start = jnp.minimum(block_start, padded_ends[-1] - MOE_BLOCK)
    block_e = jnp.sum((padded_ends[None, :] <= block_start[:, None]).astype(jnp.int32), axis=1)
    block_e = jnp.minimum(block_e, N_EXPERTS - 1)
    return dict(pos1=pos1, pos2=pos2, padlo=(padded_starts + counts).astype(jnp.int32),
                padhi=padded_ends.astype(jnp.int32), block_e=block_e,
                n_valid=n_valid.reshape(1), n_slots=n_blocks * MOE_BLOCK)


def _block_diag(w):
    nb, r, c = w.shape
    eye = jnp.eye(nb, dtype=w.dtype)
    return (w[:, :, None, :] * eye[:, None, :, None]).reshape(nb * r, nb * c)


def kernel(x, norm1_g, w_in, q_norm_g, k_norm_g, lru_conv_w, lru_conv_b, lru_wa, lru_ba, lru_wx, lru_bx, lru_lambda, conf_dw_w, conf_dw_b, conf_ln_g, conf_ln_b, conf_pw_w, conf_pw_b, out_norm_g, w_out, norm2_g, router_coarse, router_fine, exp_w_gate, exp_w_up, exp_w_down):
    bsz, seq, d = x.shape
    n = bsz * seq
    depth = w_in.shape[0]
    assert d == D_MODEL and n % ROW_TILE == 0 and seq % MIX_TILE == 0 and seq % ATTN_BQ == 0

    heads = D_SB // HEAD_DIM
    head_of = jnp.arange(D_SB, dtype=jnp.int32) // HEAD_DIM
    gsum = (head_of[:, None] == head_of[None, :]).astype(BF16)
    t_io = jnp.arange(ROW_TILE, dtype=jnp.int32)
    ltri = (t_io[None, :] < t_io[:, None]).astype(BF16)
    row = lambda v: v.reshape(1, -1)

    xf = x.reshape(n, d)
    for l in range(depth):
        qkv, rest = _inproj(xf, row(norm1_g[l]), w_in[l].astype(BF16),
                            row(jnp.tile(q_norm_g[l], heads)), row(jnp.tile(k_norm_g[l], heads)), gsum)
        ya = _attention(qkv, bsz, seq).reshape(n, D_SB)
        wgates = jnp.concatenate([_block_diag(lru_wa[l]), _block_diag(lru_wx[l])], axis=1).astype(BF16)
        ybc = _mixers(rest, bsz, seq, lru_conv_w[l], row(lru_conv_b[l]), wgates,
                      row(lru_ba[l]), row(lru_bx[l]), row(lru_lambda[l]),
                      conf_dw_w[l], row(conf_dw_b[l]), row(conf_ln_g[l]), row(conf_ln_b[l]),
                      conf_pw_w[l].astype(BF16), row(conf_pw_b[l])).reshape(n, D_LRU + D_CONF)
        wr = jnp.concatenate([router_coarse[l], router_fine[l]], axis=1)
        wr = jnp.pad(wr, ((0, 0), (0, LANES - wr.shape[1]))).astype(BF16)
        x1, h2, route, route_t, counts = _outproj(ya, ybc, xf, row(out_norm_g[l]), w_out[l].astype(BF16),
                                                  row(norm2_g[l]), wr, ltri)
        plan = _slot_plan(route_t, counts, n)
        xs = _dispatch(plan["pos1"], plan["pos2"], plan["padlo"], plan["padhi"], plan["n_valid"], h2,
                       plan["n_slots"])
        ys = _experts(plan["block_e"], plan["n_valid"], xs, exp_w_gate[l].astype(BF16),
                      exp_w_up[l].astype(BF16), exp_w_down[l].astype(BF16))
        xf = _combine(plan["pos1"], plan["pos2"], ys, x1, route)
    return xf.reshape(bsz, seq, d)
```

```python
import functools

import jax
import jax.numpy as jnp
from jax import lax
from jax.experimental import pallas as pl
from jax.experimental.pallas import tpu as pltpu

F32 = jnp.float32
BF16 = jnp.bfloat16

D_MODEL = 1024
HEAD_DIM = 64
D_SB = 512
D_LRU = 256
LRU_BLOCKS = 4
LRU_CONV_WIDTH = 4
LRU_C = 8.0
D_CONF = 256
CONF_CONV_WIDTH = 31
D_QKV = 3 * D_SB
D_REST = 2 * D_LRU + 2 * D_CONF
N_GROUPS = 4
EXPERTS_PER_GROUP = 8
N_EXPERTS = N_GROUPS * EXPERTS_PER_GROUP
D_EXPERT = 512
MOE_BLOCK = 256
EPS = 1e-6

LANES = 128
ROW_TILE = 256
DENSE_TILE = 512
ATTN_BQ = 256
ATTN_BK = 128
MIX_TILE = 512
CONF_HALO = 32
DMA_UNROLL = 8
LRU_HALO = 8
ATTN_DEAD_LOG = -110.0
VMEM_LIMIT = 48 * 1024 * 1024


def _cparams(*sem):
    return pltpu.CompilerParams(dimension_semantics=sem, vmem_limit_bytes=VMEM_LIMIT)


def _rms(y):
    return y * lax.rsqrt(jnp.mean(y * y, axis=-1, keepdims=True) + EPS)


def _inproj_kernel(x_ref, g_ref, w_ref, qg_ref, kg_ref, gsum_ref, qkv_ref, rest_ref):
    h = (_rms(x_ref[...]) * g_ref[...]).astype(BF16)

    def head_norm(p, gain):
        s = jnp.dot((p * p).astype(BF16), gsum_ref[...], preferred_element_type=F32)
        return p * lax.rsqrt(s * (1.0 / HEAD_DIM) + EPS) * gain

    q = jnp.dot(h, w_ref[:, 0:D_SB], preferred_element_type=F32)
    qkv_ref[:, 0:D_SB] = head_norm(q, qg_ref[...]).astype(BF16)
    k = jnp.dot(h, w_ref[:, D_SB:2 * D_SB], preferred_element_type=F32)
    qkv_ref[:, D_SB:2 * D_SB] = head_norm(k, kg_ref[...]).astype(BF16)
    v = jnp.dot(h, w_ref[:, 2 * D_SB:D_QKV], preferred_element_type=F32)
    qkv_ref[:, 2 * D_SB:D_QKV] = v.astype(BF16)
    rest_ref[...] = jnp.dot(h, w_ref[:, D_QKV:], preferred_element_type=F32)


def _inproj(x, g, w, qg, kg, gsum):
    n = x.shape[0]
    tm = DENSE_TILE
    const = lambda i: (0, 0)
    return pl.pallas_call(
        _inproj_kernel,
        grid=(n // tm,),
        in_specs=[
            pl.BlockSpec((tm, D_MODEL), lambda i: (i, 0)),
            pl.BlockSpec((1, D_MODEL), const),
            pl.BlockSpec((D_MODEL, D_QKV + D_REST), const),
            pl.BlockSpec((1, D_SB), const),
            pl.BlockSpec((1, D_SB), const),
            pl.BlockSpec((D_SB, D_SB), const),
        ],
        out_specs=[
            pl.BlockSpec((tm, D_QKV), lambda i: (i, 0)),
            pl.BlockSpec((tm, D_REST), lambda i: (i, 0)),
        ],
        out_shape=[
            jax.ShapeDtypeStruct((n, D_QKV), BF16),
            jax.ShapeDtypeStruct((n, D_REST), F32),
        ],
        compiler_params=_cparams("arbitrary"),
        name="inproj",
    )(x, g, w, qg, kg, gsum)


def _attn_kernel(q_ref, k_ref, v_ref, o_ref, z0, z1, lbc0, lbc1, hl0, hl1, c0, c1, cm0, cm1, w0, w1, acc_ref):
    bq, bk = ATTN_BQ, ATTN_BK
    heads = (0, 1)
    z_refs, lbc_refs, hilo_refs, c_refs, cmax_refs = (z0, z1), (lbc0, lbc1), (hl0, hl1), (c0, c1), (cm0, cm1)
    w_refs = (w0, w1)
    i = pl.program_id(2)
    q = q_ref[0] * jnp.asarray(HEAD_DIM ** -0.5, BF16)
    first = lax.broadcasted_iota(jnp.int32, (bk, LANES), 1) < HEAD_DIM
    t_pos = lax.broadcasted_iota(jnp.int32, (bq, bk), 0) + i * bq
    s_io = lax.broadcasted_iota(jnp.int32, (bq, bk), 1)
    earlier = (lax.broadcasted_iota(jnp.int32, (bk, bk), 0)
               > lax.broadcasted_iota(jnp.int32, (bk, bk), 1))
    neg_later = jnp.where(earlier, -1.0, 0.0).astype(BF16)
    neg_later2 = jnp.concatenate([neg_later, neg_later], axis=0)

    def rows(m):
        return pl.ds(pl.multiple_of(jnp.maximum(m, 0) * bk, bk), bk)

    def causal(m):
        return (s_io + m * bk) < t_pos

    def per_head(x):
        zero = jnp.zeros_like(x)
        return jnp.concatenate([jnp.where(first, x, zero), jnp.where(first, zero, x)], axis=0)

    def stage0(m, s):
        z = lax.dot_general(q, per_head(k_ref[0, rows(m), :]), (((1,), (1,)), ((), ())),
                            preferred_element_type=F32)
        for h in heads:
            z_refs[s][h] = z[:, h * bk:(h + 1) * bk]

    def stage1(m, s, masked):
        mask = causal(m) if masked else None
        for h in heads:
            z = z_refs[s][h]
            c = c_refs[s][h]
            drop = jnp.maximum(z, 0.0) + jnp.log(1.0 + jnp.exp(-jnp.abs(z)))
            lbc_refs[s][h] = (z - drop) + c
            if masked:
                drop = jnp.where(mask, drop, 0.0)
            hi = drop.astype(BF16)
            lo = (drop - hi.astype(F32)).astype(BF16)
            hilo_refs[s][h] = jnp.concatenate([hi, lo], axis=1)
            c_next = c - jnp.sum(drop, axis=-1, keepdims=True)
            c_refs[1 - s][h] = c_next
            cmax_refs[1 - s][h] = jnp.max(c_next.reshape(bq // 8, 8, LANES), axis=0)

    def stage2_sums(s):
        return [jnp.dot(hilo_refs[s][h], neg_later2, preferred_element_type=F32) for h in heads]

    def stage2_apply(m, s, within, masked):
        mask = causal(m) if masked else None
        ws = []
        for h in heads:
            w = jnp.exp(lbc_refs[s][h] + within[h])
            if masked:
                w = jnp.where(mask, w, 0.0)
            ws.append(w.astype(BF16))
        w_refs[s][...] = jnp.concatenate(ws, axis=1)

    def stage3(m, s):
        acc_ref[...] += jnp.dot(w_refs[s][...], per_head(v_ref[0, rows(m), :]), preferred_element_type=F32)

    def alive(m, s):
        return jnp.logical_and(m >= 0, jnp.max(cmax_refs[s][...]) > ATTN_DEAD_LOG)

    top = (i + 1) * (bq // bk) - 1
    acc_ref[...] = jnp.zeros_like(acc_ref)
    c_refs[1][...] = jnp.zeros((2, bq, LANES), F32)

    def step(m, s, masked_next=False, masked=False, has_prev=True):
        live_next = alive(m - 1, 1 - s)
        if has_prev:
            stage3(m + 1, 1 - s)
        within = stage2_sums(s)
        stage0(m - 2, s)
        stage1(m - 1, 1 - s, masked_next)
        stage2_apply(m, s, within, masked)
        return live_next

    stage0(top, 1)
    stage0(top - 1, 0)
    stage1(top, 1, True)
    step(top, 1, masked_next=True, masked=True, has_prev=False)
    step(top - 1, 0, masked=True)

    def body(st):
        m, _ = st
        live_next = lax.cond((m & 1) == 0, functools.partial(step, m, 0), functools.partial(step, m, 1))
        return m - 1, live_next

    m_end, _ = lax.while_loop(lambda st: st[1], body, (top - 2, alive(top - 2, 1)))
    for s in (0, 1):
        pl.when(((m_end + 1) & 1) == s)(functools.partial(stage3, m_end + 1, s))
    o_ref[0] = acc_ref[...]


def _attention(qkv, bsz, seq):
    bq, bk = ATTN_BQ, ATTN_BK
    assert bq == 2 * bk
    pairs = D_SB // LANES
    qkv3 = qkv.reshape(bsz, seq, D_QKV)
    return pl.pallas_call(
        _attn_kernel,
        grid=(bsz, pairs, seq // bq),
        in_specs=[
            pl.BlockSpec((1, bq, LANES), lambda b, p, i: (b, i, p)),
            pl.BlockSpec((1, seq, LANES), lambda b, p, i: (b, 0, pairs + p)),
            pl.BlockSpec((1, seq, LANES), lambda b, p, i: (b, 0, 2 * pairs + p)),
        ],
        out_specs=pl.BlockSpec((1, bq, LANES), lambda b, p, i: (b, i, p)),
        out_shape=jax.ShapeDtypeStruct((bsz, seq, D_SB), F32),
        scratch_shapes=(
            [pltpu.VMEM((2, bq, bk), F32)] * 2
            + [pltpu.VMEM((2, bq, bk), F32)] * 2
            + [pltpu.VMEM((2, bq, 2 * bk), BF16)] * 2
            + [pltpu.VMEM((2, bq, LANES), F32)] * 2
            + [pltpu.VMEM((2, 8, LANES), F32)] * 2
            + [pltpu.VMEM((bq, 2 * bk), BF16)] * 2
            + [pltpu.VMEM((bq, LANES), F32)]
        ),
        compiler_params=_cparams("arbitrary", "arbitrary", "arbitrary"),
        name="attn",
    )(qkv3, qkv3, qkv3)


def _shift_rows(x, d, fill):
    t = x.shape[0]
    if d % 8 == 0:
        return jnp.concatenate([jnp.full((d, x.shape[1]), fill, x.dtype), x[:t - d]], axis=0)
    rows = lax.broadcasted_iota(jnp.int32, x.shape, 0)
    return jnp.where(rows < d, fill, pltpu.roll(x, d, axis=0))


def _mixers_kernel(rest_ref, lcw_ref, lcb_ref, wg_ref, ba_ref, bx_ref, lam_ref,
                   cw_ref, cb_ref, lng_ref, lnb_ref, pw_ref, pb_ref,
                   y_ref, ltail_ref, h_ref, ctail_ref, lwin_ref, cwin_ref):
    t = MIX_TILE

    @pl.when(pl.program_id(1) == 0)
    def _():
        ltail_ref[...] = jnp.zeros_like(ltail_ref)
        h_ref[...] = jnp.zeros_like(h_ref)
        ctail_ref[...] = jnp.zeros_like(ctail_ref)

    xr = rest_ref[0, :, 0:D_LRU]
    xg = rest_ref[0, :, D_LRU:2 * D_LRU]
    val = rest_ref[0, :, 2 * D_LRU:2 * D_LRU + D_CONF]
    gate = rest_ref[0, :, 2 * D_LRU + D_CONF:]

    lwin_ref[0:LRU_HALO, :] = ltail_ref[...]
    lwin_ref[LRU_HALO:, :] = xr
    ltail_ref[...] = xr[t - LRU_HALO:, :]
    conv = lcb_ref[...]
    for k in range(LRU_CONV_WIDTH):
        off = LRU_HALO - (LRU_CONV_WIDTH - 1) + k
        conv = conv + lcw_ref[k:k + 1, :] * lwin_ref[off:off + t, :]
    gates = jnp.dot(conv.astype(BF16), wg_ref[...], preferred_element_type=F32)
    gate_r = gates[:, 0:D_LRU] + ba_ref[...]
    gate_i = gates[:, D_LRU:] + bx_ref[...]
    log_a = LRU_C * jax.nn.sigmoid(gate_r) * jax.nn.log_sigmoid(lam_ref[...])
    a = jnp.exp(log_a)
    th = jnp.tanh(log_a)
    u = jnp.sqrt(-2.0 * th / (1.0 - th)) * (jax.nn.sigmoid(gate_i) * conv)
    d = 1
    while d < t:
        u = a * _shift_rows(u, d, 0.0) + u
        a = a * _shift_rows(a, d, 1.0)
        d *= 2
    h = a * h_ref[...] + u
    h_ref[...] = h[t - 1:t, :]
    y_ref[0, :, 0:D_LRU] = h * jax.nn.gelu(xg)

    glu = val * jax.nn.sigmoid(gate)
    cwin_ref[0:CONF_HALO, :] = ctail_ref[...]
    cwin_ref[CONF_HALO:, :] = glu
    ctail_ref[...] = glu[t - CONF_HALO:, :]
    acc = cb_ref[...]
    first_off = CONF_HALO - (CONF_CONV_WIDTH - 1)
    for r in range(8):
        offs = [o for o in range(first_off, first_off + CONF_CONV_WIDTH) if o % 8 == r]
        span = t + (8 if r else 0)
        part = None
        for o in offs:
            term = cw_ref[o - first_off:o - first_off + 1, :] * cwin_ref[o - r:o - r + span, :]
            part = term if part is None else part + term
        acc = acc + part[r:r + t, :]
    mu = jnp.mean(acc, axis=-1, keepdims=True)
    cen = acc - mu
    var = jnp.mean(cen * cen, axis=-1, keepdims=True)
    ln = cen * lax.rsqrt(var + EPS) * lng_ref[...] + lnb_ref[...]
    sw = (ln * jax.nn.sigmoid(ln)).astype(BF16)
    y_ref[0, :, D_LRU:] = jnp.dot(sw, pw_ref[...], preferred_element_type=F32) + pb_ref[...]


def _mixers(rest, bsz, seq, lcw, lcb, wgates, ba, bx, lam, cw, cb, lng, lnb, pw, pb):
    t = MIX_TILE
    rest3 = rest.reshape(bsz, seq, D_REST)
    const = lambda b, i: (0, 0)
    row = lambda d: pl.BlockSpec((1, d), const)
    return pl.pallas_call(
        _mixers_kernel,
        grid=(bsz, seq // t),
        in_specs=[
            pl.BlockSpec((1, t, D_REST), lambda b, i: (b, i, 0)),
            pl.BlockSpec((LRU_CONV_WIDTH, D_LRU), const), row(D_LRU),
            pl.BlockSpec((D_LRU, 2 * D_LRU), const), row(D_LRU), row(D_LRU), row(D_LRU),
            pl.BlockSpec((CONF_CONV_WIDTH, D_CONF), const), row(D_CONF),
            row(D_CONF), row(D_CONF),
            pl.BlockSpec((D_CONF, D_CONF), const), row(D_CONF),
        ],
        out_specs=pl.BlockSpec((1, t, D_LRU + D_CONF), lambda b, i: (b, i, 0)),
        out_shape=jax.ShapeDtypeStruct((bsz, seq, D_LRU + D_CONF), F32),
        scratch_shapes=[
            pltpu.VMEM((LRU_HALO, D_LRU), F32),
            pltpu.VMEM((1, D_LRU), F32),
            pltpu.VMEM((CONF_HALO, D_CONF), F32),
            pltpu.VMEM((LRU_HALO + t, D_LRU), F32),
            pltpu.VMEM((CONF_HALO + t, D_CONF), F32),
        ],
        compiler_params=_cparams("arbitrary", "arbitrary"),
        name="mixers",
    )(rest3, lcw, lcb, wgates, ba, bx, lam, cw, cb, lng, lnb, pw, pb)


ROUTE_G1, ROUTE_G2, ROUTE_E1, ROUTE_E2, ROUTE_R1, ROUTE_R2 = range(6)
ROUTE_ROWS = 8
ROUTER_FINE_COL0 = N_GROUPS


def _outproj_kernel(ya_ref, ybc_ref, x_ref, gout_ref, wout_ref, g2_ref, wr_ref, ltri_ref,
                    x1_ref, h2_ref, route_ref, route_t_ref, counts_ref, cnt_ref):
    @pl.when(pl.program_id(0) == 0)
    def _():
        cnt_ref[...] = jnp.zeros_like(cnt_ref)

    y = jnp.concatenate(
        [_rms(ya_ref[...]), _rms(ybc_ref[:, 0:D_LRU]), _rms(ybc_ref[:, D_LRU:])], axis=1)
    y = (y * gout_ref[...]).astype(BF16)
    x1 = x_ref[...] + jnp.dot(y, wout_ref[...], preferred_element_type=F32)
    x1_ref[...] = x1
    h2 = _rms(x1) * g2_ref[...]
    h2_ref[...] = h2
    lf = jnp.dot(h2.astype(BF16), wr_ref[...], preferred_element_type=F32)

    tm = lf.shape[0]
    col_i = lax.broadcasted_iota(jnp.int32, (tm, LANES), 1)
    col = col_i.astype(F32)
    neg = -jnp.inf
    big = float(LANES)
    is_c = col_i < N_GROUPS
    lc = jnp.where(is_c, lf, neg)
    mc = jnp.max(lc, axis=-1, keepdims=True)
    grp = jnp.min(jnp.where(lc == mc, col, big), axis=-1, keepdims=True)
    w_grp = 1.0 / jnp.sum(jnp.where(is_c, jnp.exp(lf - mc), 0.0), axis=-1, keepdims=True)
    lo = ROUTER_FINE_COL0 + EXPERTS_PER_GROUP * grp
    in_grp = jnp.logical_and(col >= lo, col < lo + EXPERTS_PER_GROUP)
    l1 = jnp.where(in_grp, lf, neg)
    v1 = jnp.max(l1, axis=-1, keepdims=True)
    i1 = jnp.min(jnp.where(l1 == v1, col, big), axis=-1, keepdims=True)
    sel1 = col == i1
    l2 = jnp.where(sel1, neg, l1)
    v2 = jnp.max(l2, axis=-1, keepdims=True)
    i2 = jnp.min(jnp.where(l2 == v2, col, big), axis=-1, keepdims=True)
    sel2 = col == i2
    e21 = jnp.exp(v2 - v1)
    g1 = w_grp / (1.0 + e21)
    g2 = w_grp * e21 / (1.0 + e21)

    oh1 = jnp.where(sel1, 1.0, 0.0)
    oh2 = jnp.where(sel2, 1.0, 0.0)
    before1 = jnp.dot(ltri_ref[...], oh1.astype(BF16), preferred_element_type=F32)
    before2 = jnp.dot(ltri_ref[...], oh2.astype(BF16), preferred_element_type=F32)
    cnt = cnt_ref[...]
    tot1 = jnp.sum(oh1, axis=0, keepdims=True)
    tot2 = jnp.sum(oh2, axis=0, keepdims=True)
    r1 = jnp.sum(jnp.where(sel1, before1 + cnt, 0.0), axis=-1, keepdims=True)
    r2 = jnp.sum(jnp.where(sel2, before2 + cnt + tot1, 0.0), axis=-1, keepdims=True)
    cnt_ref[...] = cnt + tot1 + tot2

    out = jnp.zeros((tm, LANES), F32)
    for lane_id, val in ((ROUTE_G1, g1), (ROUTE_G2, g2),
                         (ROUTE_E1, i1 - ROUTER_FINE_COL0), (ROUTE_E2, i2 - ROUTER_FINE_COL0),
                         (ROUTE_R1, r1), (ROUTE_R2, r2)):
        out = jnp.where(col_i == lane_id, val, out)
    route_ref[...] = out
    route_t_ref[...] = out.T[0:ROUTE_ROWS, :]
    counts_ref[...] = cnt_ref[...]


def _outproj(ya, ybc, x, gout, wout, g2, wr, ltri):
    n = x.shape[0]
    tm = DENSE_TILE
    const = lambda i: (0, 0)
    tile = lambda d: pl.BlockSpec((tm, d), lambda i: (i, 0))
    return pl.pallas_call(
        _outproj_kernel,
        grid=(n // tm,),
        in_specs=[
            tile(D_SB), tile(D_LRU + D_CONF), tile(D_MODEL),
            pl.BlockSpec((1, D_MODEL), const),
            pl.BlockSpec((D_MODEL, D_MODEL), const),
            pl.BlockSpec((1, D_MODEL), const),
            pl.BlockSpec((D_MODEL, LANES), const),
            pl.BlockSpec((tm, tm), const),
        ],
        out_specs=[tile(D_MODEL), tile(D_MODEL), tile(LANES),
                   pl.BlockSpec((ROUTE_ROWS, tm), lambda i: (0, i)),
                   pl.BlockSpec((1, LANES), const)],
        out_shape=[
            jax.ShapeDtypeStruct((n, D_MODEL), F32),
            jax.ShapeDtypeStruct((n, D_MODEL), F32),
            jax.ShapeDtypeStruct((n, LANES), F32),
            jax.ShapeDtypeStruct((ROUTE_ROWS, n), F32),
            jax.ShapeDtypeStruct((1, LANES), F32),
        ],
        scratch_shapes=[pltpu.VMEM((1, LANES), F32)],
        compiler_params=_cparams("arbitrary"),
        name="outproj",
    )(ya, ybc, x, gout, wout, g2, wr, ltri)


def _row_copy(src_ref, src_row, dst_ref, dst_row, sem):
    return pltpu.make_async_copy(src_ref.at[pl.ds(src_row, 1)], dst_ref.at[pl.ds(dst_row, 1)], sem)


def _dispatch_kernel(pos1_ref, pos2_ref, padlo_ref, padhi_ref, nv_ref, h2_ref, xs_ref, zero_ref, sem, zsem):
    tm = ROW_TILE
    i = pl.program_id(0)
    base = i * tm

    @pl.when(i == 0)
    def _():
        zero_ref[...] = jnp.zeros_like(zero_ref)

        def per_expert(e, _):
            def start(r, c):
                _row_copy(zero_ref, 0, xs_ref, r, zsem).start()
                return c
            lax.fori_loop(padlo_ref[e], padhi_ref[e], start, 0)

            def wait(r, c):
                _row_copy(zero_ref, 0, xs_ref, r, zsem).wait()
                return c
            lax.fori_loop(padlo_ref[e], padhi_ref[e], wait, 0)
            return 0
        lax.fori_loop(0, N_EXPERTS, per_expert, 0)

        def idle_copy(b):
            rows = pl.ds(pl.multiple_of(b * MOE_BLOCK, MOE_BLOCK), MOE_BLOCK)
            return pltpu.make_async_copy(zero_ref, xs_ref.at[rows], zsem)

        def idle_start(b, c):
            idle_copy(b).start()
            return c

        def idle_wait(b, c):
            idle_copy(b).wait()
            return c
        n_blocks = xs_ref.shape[0] // MOE_BLOCK
        lax.fori_loop(nv_ref[0], n_blocks, idle_start, 0)
        lax.fori_loop(nv_ref[0], n_blocks, idle_wait, 0)

    def start(r, c):
        _row_copy(h2_ref, r, xs_ref, pos1_ref[base + r], sem).start()
        _row_copy(h2_ref, r, xs_ref, pos2_ref[base + r], sem).start()
        return c
    lax.fori_loop(0, tm, start, 0, unroll=DMA_UNROLL)

    def wait(r, c):
        _row_copy(h2_ref, r, xs_ref, pos1_ref[base + r], sem).wait()
        _row_copy(h2_ref, r, xs_ref, pos2_ref[base + r], sem).wait()
        return c
    lax.fori_loop(0, tm, wait, 0, unroll=DMA_UNROLL)


def _dispatch(pos1, pos2, padlo, padhi, n_valid, h2, n_slots):
    n = h2.shape[0]
    tm = ROW_TILE
    return pl.pallas_call(
        _dispatch_kernel,
        grid_spec=pltpu.PrefetchScalarGridSpec(
            num_scalar_prefetch=5,
            grid=(n // tm,),
            in_specs=[pl.BlockSpec((tm, D_MODEL), lambda i, *_: (i, 0))],
            out_specs=pl.BlockSpec(memory_space=pl.ANY),
            scratch_shapes=[
                pltpu.VMEM((MOE_BLOCK, D_MODEL), F32),
                pltpu.SemaphoreType.DMA(()),
                pltpu.SemaphoreType.DMA(()),
            ],
        ),
        out_shape=jax.ShapeDtypeStruct((n_slots, D_MODEL), F32),
        compiler_params=_cparams("arbitrary"),
        name="dispatch",
    )(pos1, pos2, padlo, padhi, n_valid, h2)


def _experts_kernel(be_ref, nv_ref, xs_ref, wg_ref, wu_ref, wd_ref, ys_ref, wg_bf, wu_bf, wd_bf):
    i = pl.program_id(0)
    live = i < nv_ref[0]

    new_expert = jnp.logical_or(i == 0, be_ref[i] != be_ref[jnp.maximum(i - 1, 0)])

    @pl.when(jnp.logical_and(live, new_expert))
    def _():
        wg_bf[...] = wg_ref[0, 0].astype(BF16)
        wu_bf[...] = wu_ref[0, 0].astype(BF16)
        wd_bf[...] = wd_ref[0, 0].astype(BF16)

    @pl.when(live)
    def _():
        x = xs_ref[...].astype(BF16)
        g = jnp.dot(x, wg_bf[...], preferred_element_type=F32)
        u = jnp.dot(x, wu_bf[...], preferred_element_type=F32)
        a = (g * jax.nn.sigmoid(g) * u).astype(BF16)
        ys_ref[...] = jnp.dot(a, wd_bf[...], preferred_element_type=F32)

    @pl.when(jnp.logical_not(live))
    def _():
        ys_ref[...] = jnp.zeros_like(ys_ref)


def _experts(block_e, n_valid, xs, wg, wu, wd, layer):
    n_slots = xs.shape[0]
    n_blocks = n_slots // MOE_BLOCK
    weight = lambda i, be, nv: (layer, be[i], 0, 0)
    return pl.pallas_call(
        _experts_kernel,
        grid_spec=pltpu.PrefetchScalarGridSpec(
            num_scalar_prefetch=2,
            grid=(n_blocks,),
            in_specs=[
                pl.BlockSpec((MOE_BLOCK, D_MODEL), lambda i, be, nv: (jnp.minimum(i, nv[0] - 1), 0)),
                pl.BlockSpec((1, 1, D_MODEL, D_EXPERT), weight),
                pl.BlockSpec((1, 1, D_MODEL, D_EXPERT), weight),
                pl.BlockSpec((1, 1, D_EXPERT, D_MODEL), weight),
            ],
            out_specs=pl.BlockSpec((MOE_BLOCK, D_MODEL), lambda i, be, nv: (i, 0)),
            scratch_shapes=[
                pltpu.VMEM((D_MODEL, D_EXPERT), BF16),
                pltpu.VMEM((D_MODEL, D_EXPERT), BF16),
                pltpu.VMEM((D_EXPERT, D_MODEL), BF16),
            ],
        ),
        out_shape=jax.ShapeDtypeStruct((n_slots, D_MODEL), F32),
        compiler_params=_cparams("arbitrary"),
        name="experts",
    )(block_e, n_valid, xs, wg, wu, wd)


def _combine_kernel(pos1_ref, pos2_ref, ys_ref, x1_ref, route_ref, out_ref, buf_ref, sems):
    tm = ROW_TILE
    i = pl.program_id(0)
    n_steps = pl.num_programs(0)

    def gather(step, slot, action):
        base = step * tm

        def body(r, c):
            for which, pos_ref in enumerate((pos1_ref, pos2_ref)):
                cp = pltpu.make_async_copy(
                    ys_ref.at[pl.ds(pos_ref[base + r], 1)],
                    buf_ref.at[slot, which, pl.ds(r, 1)],
                    sems.at[slot])
                if action == "start":
                    cp.start()
                else:
                    cp.wait()
            return c
        lax.fori_loop(0, tm, body, 0, unroll=DMA_UNROLL)

    slot = lax.rem(i, 2)

    @pl.when(i == 0)
    def _():
        gather(0, 0, "start")

    @pl.when(i + 1 < n_steps)
    def _():
        gather(i + 1, 1 - slot, "start")

    gather(i, slot, "wait")
    col = lax.broadcasted_iota(jnp.int32, (tm, LANES), 1)
    route = route_ref[...]
    g1 = jnp.sum(jnp.where(col == ROUTE_G1, route, 0.0), axis=-1, keepdims=True)
    g2 = jnp.sum(jnp.where(col == ROUTE_G2, route, 0.0), axis=-1, keepdims=True)
    out_ref[...] = x1_ref[...] + (buf_ref[slot, 0] * g1 + buf_ref[slot, 1] * g2)


def _combine(pos1, pos2, ys, x1, route):
    n = x1.shape[0]
    tm = ROW_TILE
    return pl.pallas_call(
        _combine_kernel,
        grid_spec=pltpu.PrefetchScalarGridSpec(
            num_scalar_prefetch=2,
            grid=(n // tm,),
            in_specs=[
                pl.BlockSpec(memory_space=pl.ANY),
                pl.BlockSpec((tm, D_MODEL), lambda i, *_: (i, 0)),
                pl.BlockSpec((tm, LANES), lambda i, *_: (i, 0)),
            ],
            out_specs=pl.BlockSpec((tm, D_MODEL), lambda i, *_: (i, 0)),
            scratch_shapes=[
                pltpu.VMEM((2, 2, tm, D_MODEL), F32),
                pltpu.SemaphoreType.DMA((2,)),
            ],
        ),
        out_shape=jax.ShapeDtypeStruct((n, D_MODEL), F32),
        compiler_params=_cparams("arbitrary"),
        name="combine",
    )(pos1, pos2, ys, x1, route)


def _slot_plan(route_t, counts_row, n_tok):
    e1 = route_t[ROUTE_E1].astype(jnp.int32)
    e2 = route_t[ROUTE_E2].astype(jnp.int32)
    r1 = route_t[ROUTE_R1].astype(jnp.int32)
    r2 = route_t[ROUTE_R2].astype(jnp.int32)
    experts = jnp.arange(N_EXPERTS, dtype=jnp.int32)
    counts = counts_row[0, ROUTER_FINE_COL0:ROUTER_FINE_COL0 + N_EXPERTS].astype(jnp.int32)
    padded = (counts + MOE_BLOCK - 1) // MOE_BLOCK * MOE_BLOCK
    padded_ends = jnp.cumsum(padded)
    padded_starts = padded_ends - padded

    def start_of(e):
        return jnp.sum(jnp.where(e[None, :] == experts[:, None], padded_starts[:, None], 0), axis=0)

    pos1 = start_of(e1) + r1
    pos2 = start_of(e2) + r2
    n_blocks = -(-(2 * n_tok) // MOE_BLOCK) + N_EXPERTS
    n_valid = (padded_ends[-1] // MOE_BLOCK).astype(jnp.int32)
    block_start = jnp.arange(n_blocks, dtype=jnp.int32) * MOE_BLOCK
    block_start = jnp.minimum(block_start, padded_ends[-1] - MOE_BLOCK)
    block_e = jnp.sum((padded_ends[None, :] <= block_start[:, None]).astype(jnp.int32), axis=1)
    block_e = jnp.minimum(block_e, N_EXPERTS - 1)
    return dict(pos1=pos1, pos2=pos2, padlo=(padded_starts + counts).astype(jnp.int32),
                padhi=padded_ends.astype(jnp.int32), block_e=block_e,
                n_valid=n_valid.reshape(1), n_slots=n_blocks * MOE_BLOCK)


def _block_diag(w):
    nb, r, c = w.shape
    eye = jnp.eye(nb, dtype=w.dtype)
    return (w[:, :, None, :] * eye[:, None, :, None]).reshape(nb * r, nb * c)


def kernel(x, norm1_g, w_in, q_norm_g, k_norm_g, lru_conv_w, lru_conv_b, lru_wa, lru_ba, lru_wx, lru_bx, lru_lambda, conf_dw_w, conf_dw_b, conf_ln_g, conf_ln_b, conf_pw_w, conf_pw_b, out_norm_g, w_out, norm2_g, router_coarse, router_fine, exp_w_gate, exp_w_up, exp_w_down):
    bsz, seq, d = x.shape
    n = bsz * seq
    depth = w_in.shape[0]
    assert d == D_MODEL and n % DENSE_TILE == 0 and n % ROW_TILE == 0
    assert seq % MIX_TILE == 0 and seq % ATTN_BQ == 0

    heads = D_SB // HEAD_DIM
    head_of = jnp.arange(D_SB, dtype=jnp.int32) // HEAD_DIM
    gsum = (head_of[:, None] == head_of[None, :]).astype(BF16)
    t_io = jnp.arange(DENSE_TILE, dtype=jnp.int32)
    ltri = (t_io[None, :] < t_io[:, None]).astype(BF16)
    row = lambda v: v.reshape(1, -1)

    xf = x.reshape(n, d)
    for l in range(depth):
        qkv, rest = _inproj(xf, row(norm1_g[l]), w_in[l].astype(BF16),
                            row(jnp.tile(q_norm_g[l], heads)), row(jnp.tile(k_norm_g[l], heads)), gsum)
        ya = _attention(qkv, bsz, seq).reshape(n, D_SB)
        wgates = jnp.concatenate([_block_diag(lru_wa[l]), _block_diag(lru_wx[l])], axis=1).astype(BF16)
        ybc = _mixers(rest, bsz, seq, lru_conv_w[l], row(lru_conv_b[l]), wgates,
                      row(lru_ba[l]), row(lru_bx[l]), row(lru_lambda[l]),
                      conf_dw_w[l], row(conf_dw_b[l]), row(conf_ln_g[l]), row(conf_ln_b[l]),
                      conf_pw_w[l].astype(BF16), row(conf_pw_b[l])).reshape(n, D_LRU + D_CONF)
        wr = jnp.concatenate([router_coarse[l], router_fine[l]], axis=1)
        wr = jnp.pad(wr, ((0, 0), (0, LANES - wr.shape[1]))).astype(BF16)
        x1, h2, route, route_t, counts = _outproj(ya, ybc, xf, row(out_norm_g[l]), w_out[l].astype(BF16),
                                                  row(norm2_g[l]), wr, ltri)
        plan = _slot_plan(route_t, counts, n)
        xs = _dispatch(plan["pos1"], plan["pos2"], plan["padlo"], plan["padhi"], plan["n_valid"], h2,
                       plan["n_slots"])
        ys = _experts(plan["block_e"], plan["n_valid"], xs, exp_w_gate, exp_w_up, exp_w_down, l)
        xf = _combine(plan["pos1"], plan["pos2"], ys, x1, route)
    return xf.reshape(bsz, seq, d)
```

```python
import functools

import jax
import jax.numpy as jnp
from jax import lax
from jax.experimental import pallas as pl
from jax.experimental.pallas import tpu as pltpu

F32 = jnp.float32
BF16 = jnp.bfloat16

D_MODEL = 1024
HEAD_DIM = 64
D_SB = 512
D_LRU = 256
LRU_BLOCKS = 4
LRU_CONV_WIDTH = 4
LRU_C = 8.0
D_CONF = 256
CONF_CONV_WIDTH = 31
D_QKV = 3 * D_SB
D_REST = 2 * D_LRU + 2 * D_CONF
N_GROUPS = 4
EXPERTS_PER_GROUP = 8
N_EXPERTS = N_GROUPS * EXPERTS_PER_GROUP
D_EXPERT = 512
MOE_BLOCK = 256
EPS = 1e-6

LANES = 128
ROW_TILE = 256
DENSE_TILE = 512
ATTN_BQ = 256
ATTN_BK = 128
ATTN_TILES = 4
MIX_TILE = 512
CONF_HALO = 32
DMA_UNROLL = 8
LRU_HALO = 8
ATTN_DEAD_LOG = -110.0
VMEM_LIMIT = 48 * 1024 * 1024


def _cparams(*sem):
    return pltpu.CompilerParams(dimension_semantics=sem, vmem_limit_bytes=VMEM_LIMIT)


def _rms(y):
    return y * lax.rsqrt(jnp.mean(y * y, axis=-1, keepdims=True) + EPS)


def _inproj_kernel(x_ref, g_ref, w_ref, qg_ref, kg_ref, gsum_ref, qkv_ref, rest_ref):
    h = (_rms(x_ref[...]) * g_ref[...]).astype(BF16)

    def head_norm(p, gain):
        s = jnp.dot((p * p).astype(BF16), gsum_ref[...], preferred_element_type=F32)
        return p * lax.rsqrt(s * (1.0 / HEAD_DIM) + EPS) * gain

    q = jnp.dot(h, w_ref[:, 0:D_SB], preferred_element_type=F32)
    qkv_ref[:, 0:D_SB] = head_norm(q, qg_ref[...]).astype(BF16)
    k = jnp.dot(h, w_ref[:, D_SB:2 * D_SB], preferred_element_type=F32)
    qkv_ref[:, D_SB:2 * D_SB] = head_norm(k, kg_ref[...]).astype(BF16)
    v = jnp.dot(h, w_ref[:, 2 * D_SB:D_QKV], preferred_element_type=F32)
    qkv_ref[:, 2 * D_SB:D_QKV] = v.astype(BF16)
    rest_ref[...] = jnp.dot(h, w_ref[:, D_QKV:], preferred_element_type=F32)


def _inproj(x, g, w, qg, kg, gsum):
    n = x.shape[0]
    tm = DENSE_TILE
    const = lambda i: (0, 0)
    return pl.pallas_call(
        _inproj_kernel,
        grid=(n // tm,),
        in_specs=[
            pl.BlockSpec((tm, D_MODEL), lambda i: (i, 0)),
            pl.BlockSpec((1, D_MODEL), const),
            pl.BlockSpec((D_MODEL, D_QKV + D_REST), const),
            pl.BlockSpec((1, D_SB), const),
            pl.BlockSpec((1, D_SB), const),
            pl.BlockSpec((D_SB, D_SB), const),
        ],
        out_specs=[
            pl.BlockSpec((tm, D_QKV), lambda i: (i, 0)),
            pl.BlockSpec((tm, D_REST), lambda i: (i, 0)),
        ],
        out_shape=[
            jax.ShapeDtypeStruct((n, D_QKV), BF16),
            jax.ShapeDtypeStruct((n, D_REST), F32),
        ],
        compiler_params=_cparams("arbitrary"),
        name="inproj",
    )(x, g, w, qg, kg, gsum)


def _attn_kernel(q_ref, k_ref, v_ref, o_ref, *scratch):
    def tile(tt, carry):
        rows = pl.ds(pl.multiple_of(tt * ATTN_BQ, ATTN_BQ), ATTN_BQ)
        _attn_tile(pl.program_id(2) * ATTN_TILES + tt, q_ref.at[0, rows], k_ref, v_ref, o_ref.at[0, rows], *scratch)
        return carry
    lax.fori_loop(0, ATTN_TILES, tile, 0)


def _attn_tile(i, q_ref, k_ref, v_ref, o_ref, z0, z1, lbc0, lbc1, hl0, hl1, c0, c1, cm0, cm1, w0, w1, acc_ref):
    bq, bk = ATTN_BQ, ATTN_BK
    heads = (0, 1)
    z_refs, lbc_refs, hilo_refs, c_refs, cmax_refs = (z0, z1), (lbc0, lbc1), (hl0, hl1), (c0, c1), (cm0, cm1)
    w_refs = (w0, w1)
    q = q_ref[...] * jnp.asarray(HEAD_DIM ** -0.5, BF16)
    first = lax.broadcasted_iota(jnp.int32, (bk, LANES), 1) < HEAD_DIM
    t_pos = lax.broadcasted_iota(jnp.int32, (bq, bk), 0) + i * bq
    s_io = lax.broadcasted_iota(jnp.int32, (bq, bk), 1)
    earlier = (lax.broadcasted_iota(jnp.int32, (bk, bk), 0)
               > lax.broadcasted_iota(jnp.int32, (bk, bk), 1))
    neg_later = jnp.where(earlier, -1.0, 0.0).astype(BF16)
    neg_later2 = jnp.concatenate([neg_later, neg_later], axis=0)

    def rows(m):
        return pl.ds(pl.multiple_of(jnp.maximum(m, 0) * bk, bk), bk)

    def causal(m):
        return (s_io + m * bk) < t_pos

    def per_head(x):
        zero = jnp.zeros_like(x)
        return jnp.concatenate([jnp.where(first, x, zero), jnp.where(first, zero, x)], axis=0)

    def stage0(m, s):
        z = lax.dot_general(q, per_head(k_ref[0, rows(m), :]), (((1,), (1,)), ((), ())),
                            preferred_element_type=F32)
        for h in heads:
            z_refs[s][h] = z[:, h * bk:(h + 1) * bk]

    def stage1(m, s, masked):
        mask = causal(m) if masked else None
        for h in heads:
            z = z_refs[s][h]
            c = c_refs[s][h]
            drop = jnp.maximum(z, 0.0) + jnp.log(1.0 + jnp.exp(-jnp.abs(z)))
            lbc_refs[s][h] = (z - drop) + c
            if masked:
                drop = jnp.where(mask, drop, 0.0)
            hi = drop.astype(BF16)
            lo = (drop - hi.astype(F32)).astype(BF16)
            hilo_refs[s][h] = jnp.concatenate([hi, lo], axis=1)
            c_next = c - jnp.sum(drop, axis=-1, keepdims=True)
            c_refs[1 - s][h] = c_next
            cmax_refs[1 - s][h] = jnp.max(c_next.reshape(bq // 8, 8, LANES), axis=0)

    def stage2_sums(s):
        return [jnp.dot(hilo_refs[s][h], neg_later2, preferred_element_type=F32) for h in heads]

    def stage2_apply(m, s, within, masked):
        mask = causal(m) if masked else None
        ws = []
        for h in heads:
            w = jnp.exp(lbc_refs[s][h] + within[h])
            if masked:
                w = jnp.where(mask, w, 0.0)
            ws.append(w.astype(BF16))
        w_refs[s][...] = jnp.concatenate(ws, axis=1)

    def stage3(m, s):
        acc_ref[...] += jnp.dot(w_refs[s][...], per_head(v_ref[0, rows(m), :]), preferred_element_type=F32)

    def alive(m, s):
        return jnp.logical_and(m >= 0, jnp.max(cmax_refs[s][...]) > ATTN_DEAD_LOG)

    top = (i + 1) * (bq // bk) - 1
    acc_ref[...] = jnp.zeros_like(acc_ref)
    c_refs[1][...] = jnp.zeros((2, bq, LANES), F32)

    def step(m, s, masked_next=False, masked=False, has_prev=True):
        live_next = alive(m - 1, 1 - s)
        if has_prev:
            stage3(m + 1, 1 - s)
        within = stage2_sums(s)
        stage0(m - 2, s)
        stage1(m - 1, 1 - s, masked_next)
        stage2_apply(m, s, within, masked)
        return live_next

    stage0(top, 1)
    stage0(top - 1, 0)
    stage1(top, 1, True)
    step(top, 1, masked_next=True, masked=True, has_prev=False)
    step(top - 1, 0, masked=True)

    def body(st):
        m, _ = st
        live_next = lax.cond((m & 1) == 0, functools.partial(step, m, 0), functools.partial(step, m, 1))
        return m - 1, live_next

    m_end, _ = lax.while_loop(lambda st: st[1], body, (top - 2, alive(top - 2, 1)))
    for s in (0, 1):
        pl.when(((m_end + 1) & 1) == s)(functools.partial(stage3, m_end + 1, s))
    o_ref[...] = acc_ref[...]


def _attention(qkv, bsz, seq):
    bq, bk = ATTN_BQ, ATTN_BK
    assert bq == 2 * bk
    pairs = D_SB // LANES
    qkv3 = qkv.reshape(bsz, seq, D_QKV)
    step_rows = ATTN_TILES * bq
    return pl.pallas_call(
        _attn_kernel,
        grid=(bsz, pairs, seq // step_rows),
        in_specs=[
            pl.BlockSpec((1, step_rows, LANES), lambda b, p, i: (b, i, p)),
            pl.BlockSpec((1, seq, LANES), lambda b, p, i: (b, 0, pairs + p)),
            pl.BlockSpec((1, seq, LANES), lambda b, p, i: (b, 0, 2 * pairs + p)),
        ],
        out_specs=pl.BlockSpec((1, step_rows, LANES), lambda b, p, i: (b, i, p)),
        out_shape=jax.ShapeDtypeStruct((bsz, seq, D_SB), F32),
        scratch_shapes=(
            [pltpu.VMEM((2, bq, bk), F32)] * 2
            + [pltpu.VMEM((2, bq, bk), F32)] * 2
            + [pltpu.VMEM((2, bq, 2 * bk), BF16)] * 2
            + [pltpu.VMEM((2, bq, LANES), F32)] * 2
            + [pltpu.VMEM((2, 8, LANES), F32)] * 2
            + [pltpu.VMEM((bq, 2 * bk), BF16)] * 2
            + [pltpu.VMEM((bq, LANES), F32)]
        ),
        compiler_params=_cparams("arbitrary", "arbitrary", "arbitrary"),
        name="attn",
    )(qkv3, qkv3, qkv3)


def _shift_rows(x, d, fill):
    t = x.shape[0]
    if d % 8 == 0:
        return jnp.concatenate([jnp.full((d, x.shape[1]), fill, x.dtype), x[:t - d]], axis=0)
    rows = lax.broadcasted_iota(jnp.int32, x.shape, 0)
    return jnp.where(rows < d, fill, pltpu.roll(x, d, axis=0))


def _mixers_kernel(rest_ref, lcw_ref, lcb_ref, wg_ref, ba_ref, bx_ref, lam_ref,
                   cw_ref, cb_ref, lng_ref, lnb_ref, pw_ref, pb_ref,
                   y_ref, ltail_ref, h_ref, ctail_ref, lwin_ref, cwin_ref):
    t = MIX_TILE

    @pl.when(pl.program_id(1) == 0)
    def _():
        ltail_ref[...] = jnp.zeros_like(ltail_ref)
        h_ref[...] = jnp.zeros_like(h_ref)
        ctail_ref[...] = jnp.zeros_like(ctail_ref)

    xr = rest_ref[0, :, 0:D_LRU]
    xg = rest_ref[0, :, D_LRU:2 * D_LRU]
    val = rest_ref[0, :, 2 * D_LRU:2 * D_LRU + D_CONF]
    gate = rest_ref[0, :, 2 * D_LRU + D_CONF:]

    lwin_ref[0:LRU_HALO, :] = ltail_ref[...]
    lwin_ref[LRU_HALO:, :] = xr
    ltail_ref[...] = xr[t - LRU_HALO:, :]
    conv = lcb_ref[...]
    for k in range(LRU_CONV_WIDTH):
        off = LRU_HALO - (LRU_CONV_WIDTH - 1) + k
        conv = conv + lcw_ref[k:k + 1, :] * lwin_ref[off:off + t, :]
    gates = jnp.dot(conv.astype(BF16), wg_ref[...], preferred_element_type=F32)
    gate_r = gates[:, 0:D_LRU] + ba_ref[...]
    gate_i = gates[:, D_LRU:] + bx_ref[...]
    log_a = LRU_C * jax.nn.sigmoid(gate_r) * jax.nn.log_sigmoid(lam_ref[...])
    a = jnp.exp(log_a)
    th = jnp.tanh(log_a)
    u = jnp.sqrt(-2.0 * th / (1.0 - th)) * (jax.nn.sigmoid(gate_i) * conv)
    d = 1
    while d < t:
        u = a * _shift_rows(u, d, 0.0) + u
        a = a * _shift_rows(a, d, 1.0)
        d *= 2
    h = a * h_ref[...] + u
    h_ref[...] = h[t - 1:t, :]
    y_ref[0, :, 0:D_LRU] = h * jax.nn.gelu(xg)

    glu = val * jax.nn.sigmoid(gate)
    cwin_ref[0:CONF_HALO, :] = ctail_ref[...]
    cwin_ref[CONF_HALO:, :] = glu
    ctail_ref[...] = glu[t - CONF_HALO:, :]
    acc = cb_ref[...]
    first_off = CONF_HALO - (CONF_CONV_WIDTH - 1)
    for r in range(8):
        offs = [o for o in range(first_off, first_off + CONF_CONV_WIDTH) if o % 8 == r]
        span = t + (8 if r else 0)
        part = None
        for o in offs:
            term = cw_ref[o - first_off:o - first_off + 1, :] * cwin_ref[o - r:o - r + span, :]
            part = term if part is None else part + term
        acc = acc + part[r:r + t, :]
    mu = jnp.mean(acc, axis=-1, keepdims=True)
    cen = acc - mu
    var = jnp.mean(cen * cen, axis=-1, keepdims=True)
    ln = cen * lax.rsqrt(var + EPS) * lng_ref[...] + lnb_ref[...]
    sw = (ln * jax.nn.sigmoid(ln)).astype(BF16)
    y_ref[0, :, D_LRU:] = jnp.dot(sw, pw_ref[...], preferred_element_type=F32) + pb_ref[...]


def _mixers(rest, bsz, seq, lcw, lcb, wgates, ba, bx, lam, cw, cb, lng, lnb, pw, pb):
    t = MIX_TILE
    rest3 = rest.reshape(bsz, seq, D_REST)
    const = lambda b, i: (0, 0)
    row = lambda d: pl.BlockSpec((1, d), const)
    return pl.pallas_call(
        _mixers_kernel,
        grid=(bsz, seq // t),
        in_specs=[
            pl.BlockSpec((1, t, D_REST), lambda b, i: (b, i, 0)),
            pl.BlockSpec((LRU_CONV_WIDTH, D_LRU), const), row(D_LRU),
            pl.BlockSpec((D_LRU, 2 * D_LRU), const), row(D_LRU), row(D_LRU), row(D_LRU),
            pl.BlockSpec((CONF_CONV_WIDTH, D_CONF), const), row(D_CONF),
            row(D_CONF), row(D_CONF),
            pl.BlockSpec((D_CONF, D_CONF), const), row(D_CONF),
        ],
        out_specs=pl.BlockSpec((1, t, D_LRU + D_CONF), lambda b, i: (b, i, 0)),
        out_shape=jax.ShapeDtypeStruct((bsz, seq, D_LRU + D_CONF), F32),
        scratch_shapes=[
            pltpu.VMEM((LRU_HALO, D_LRU), F32),
            pltpu.VMEM((1, D_LRU), F32),
            pltpu.VMEM((CONF_HALO, D_CONF), F32),
            pltpu.VMEM((LRU_HALO + t, D_LRU), F32),
            pltpu.VMEM((CONF_HALO + t, D_CONF), F32),
        ],
        compiler_params=_cparams("arbitrary", "arbitrary"),
        name="mixers",
    )(rest3, lcw, lcb, wgates, ba, bx, lam, cw, cb, lng, lnb, pw, pb)


ROUTE_G1, ROUTE_G2, ROUTE_E1, ROUTE_E2, ROUTE_R1, ROUTE_R2 = range(6)
ROUTE_ROWS = 8
ROUTER_FINE_COL0 = N_GROUPS


def _outproj_kernel(ya_ref, ybc_ref, x_ref, gout_ref, wout_ref, g2_ref, wr_ref, ltri_ref,
                    x1_ref, h2_ref, route_ref, route_t_ref, counts_ref, cnt_ref):
    @pl.when(pl.program_id(0) == 0)
    def _():
        cnt_ref[...] = jnp.zeros_like(cnt_ref)

    y = jnp.concatenate(
        [_rms(ya_ref[...]), _rms(ybc_ref[:, 0:D_LRU]), _rms(ybc_ref[:, D_LRU:])], axis=1)
    y = (y * gout_ref[...]).astype(BF16)
    x1 = x_ref[...] + jnp.dot(y, wout_ref[...], preferred_element_type=F32)
    x1_ref[...] = x1
    h2 = _rms(x1) * g2_ref[...]
    h2_ref[...] = h2
    lf = jnp.dot(h2.astype(BF16), wr_ref[...], preferred_element_type=F32)

    tm = lf.shape[0]
    col_i = lax.broadcasted_iota(jnp.int32, (tm, LANES), 1)
    col = col_i.astype(F32)
    neg = -jnp.inf
    big = float(LANES)
    is_c = col_i < N_GROUPS
    lc = jnp.where(is_c, lf, neg)
    mc = jnp.max(lc, axis=-1, keepdims=True)
    grp = jnp.min(jnp.where(lc == mc, col, big), axis=-1, keepdims=True)
    w_grp = 1.0 / jnp.sum(jnp.where(is_c, jnp.exp(lf - mc), 0.0), axis=-1, keepdims=True)
    lo = ROUTER_FINE_COL0 + EXPERTS_PER_GROUP * grp
    in_grp = jnp.logical_and(col >= lo, col < lo + EXPERTS_PER_GROUP)
    l1 = jnp.where(in_grp, lf, neg)
    v1 = jnp.max(l1, axis=-1, keepdims=True)
    i1 = jnp.min(jnp.where(l1 == v1, col, big), axis=-1, keepdims=True)
    sel1 = col == i1
    l2 = jnp.where(sel1, neg, l1)
    v2 = jnp.max(l2, axis=-1, keepdims=True)
    i2 = jnp.min(jnp.where(l2 == v2, col, big), axis=-1, keepdims=True)
    sel2 = col == i2
    e21 = jnp.exp(v2 - v1)
    g1 = w_grp / (1.0 + e21)
    g2 = w_grp * e21 / (1.0 + e21)

    oh1 = jnp.where(sel1, 1.0, 0.0)
    oh2 = jnp.where(sel2, 1.0, 0.0)
    before1 = jnp.dot(ltri_ref[...], oh1.astype(BF16), preferred_element_type=F32)
    before2 = jnp.dot(ltri_ref[...], oh2.astype(BF16), preferred_element_type=F32)
    cnt = cnt_ref[...]
    tot1 = jnp.sum(oh1, axis=0, keepdims=True)
    tot2 = jnp.sum(oh2, axis=0, keepdims=True)
    r1 = jnp.sum(jnp.where(sel1, before1 + cnt, 0.0), axis=-1, keepdims=True)
    r2 = jnp.sum(jnp.where(sel2, before2 + cnt + tot1, 0.0), axis=-1, keepdims=True)
    cnt_ref[...] = cnt + tot1 + tot2

    out = jnp.zeros((tm, LANES), F32)
    for lane_id, val in ((ROUTE_G1, g1), (ROUTE_G2, g2),
                         (ROUTE_E1, i1 - ROUTER_FINE_COL0), (ROUTE_E2, i2 - ROUTER_FINE_COL0),
                         (ROUTE_R1, r1), (ROUTE_R2, r2)):
        out = jnp.where(col_i == lane_id, val, out)
    route_ref[...] = out
    route_t_ref[...] = out.T[0:ROUTE_ROWS, :]
    counts_ref[...] = cnt_ref[...]


def _outproj(ya, ybc, x, gout, wout, g2, wr, ltri):
    n = x.shape[0]
    tm = DENSE_TILE
    const = lambda i: (0, 0)
    tile = lambda d: pl.BlockSpec((tm, d), lambda i: (i, 0))
    return pl.pallas_call(
        _outproj_kernel,
        grid=(n // tm,),
        in_specs=[
            tile(D_SB), tile(D_LRU + D_CONF), tile(D_MODEL),
            pl.BlockSpec((1, D_MODEL), const),
            pl.BlockSpec((D_MODEL, D_MODEL), const),
            pl.BlockSpec((1, D_MODEL), const),
            pl.BlockSpec((D_MODEL, LANES), const),
            pl.BlockSpec((tm, tm), const),
        ],
        out_specs=[tile(D_MODEL), tile(D_MODEL), tile(LANES),
                   pl.BlockSpec((ROUTE_ROWS, tm), lambda i: (0, i)),
                   pl.BlockSpec((1, LANES), const)],
        out_shape=[
            jax.ShapeDtypeStruct((n, D_MODEL), F32),
            jax.ShapeDtypeStruct((n, D_MODEL), F32),
            jax.ShapeDtypeStruct((n, LANES), F32),
            jax.ShapeDtypeStruct((ROUTE_ROWS, n), F32),
            jax.ShapeDtypeStruct((1, LANES), F32),
        ],
        scratch_shapes=[pltpu.VMEM((1, LANES), F32)],
        compiler_params=_cparams("arbitrary"),
        name="outproj",
    )(ya, ybc, x, gout, wout, g2, wr, ltri)


def _row_copy(src_ref, src_row, dst_ref, dst_row, sem):
    return pltpu.make_async_copy(src_ref.at[pl.ds(src_row, 1)], dst_ref.at[pl.ds(dst_row, 1)], sem)


def _dispatch_kernel(pos1_ref, pos2_ref, padlo_ref, padhi_ref, nv_ref, h2_ref, xs_ref, zero_ref, sem, zsem):
    tm = ROW_TILE
    i = pl.program_id(0)
    base = i * tm

    @pl.when(i == 0)
    def _():
        zero_ref[...] = jnp.zeros_like(zero_ref)

        def per_expert(e, _):
            def start(r, c):
                _row_copy(zero_ref, 0, xs_ref, r, zsem).start()
                return c
            lax.fori_loop(padlo_ref[e], padhi_ref[e], start, 0)

            def wait(r, c):
                _row_copy(zero_ref, 0, xs_ref, r, zsem).wait()
                return c
            lax.fori_loop(padlo_ref[e], padhi_ref[e], wait, 0)
            return 0
        lax.fori_loop(0, N_EXPERTS, per_expert, 0)

        def idle_copy(b):
            rows = pl.ds(pl.multiple_of(b * MOE_BLOCK, MOE_BLOCK), MOE_BLOCK)
            return pltpu.make_async_copy(zero_ref, xs_ref.at[rows], zsem)

        def idle_start(b, c):
            idle_copy(b).start()
            return c

        def idle_wait(b, c):
            idle_copy(b).wait()
            return c
        n_blocks = xs_ref.shape[0] // MOE_BLOCK
        lax.fori_loop(nv_ref[0], n_blocks, idle_start, 0)
        lax.fori_loop(nv_ref[0], n_blocks, idle_wait, 0)

    def start(r, c):
        _row_copy(h2_ref, r, xs_ref, pos1_ref[base + r], sem).start()
        _row_copy(h2_ref, r, xs_ref, pos2_ref[base + r], sem).start(priority=1)
        return c
    lax.fori_loop(0, tm, start, 0, unroll=DMA_UNROLL)

    def wait(r, c):
        _row_copy(h2_ref, r, xs_ref, pos1_ref[base + r], sem).wait()
        _row_copy(h2_ref, r, xs_ref, pos2_ref[base + r], sem).wait()
        return c
    lax.fori_loop(0, tm, wait, 0, unroll=DMA_UNROLL)


def _dispatch(pos1, pos2, padlo, padhi, n_valid, h2, n_slots):
    n = h2.shape[0]
    tm = ROW_TILE
    return pl.pallas_call(
        _dispatch_kernel,
        grid_spec=pltpu.PrefetchScalarGridSpec(
            num_scalar_prefetch=5,
            grid=(n // tm,),
            in_specs=[pl.BlockSpec((tm, D_MODEL), lambda i, *_: (i, 0))],
            out_specs=pl.BlockSpec(memory_space=pl.ANY),
            scratch_shapes=[
                pltpu.VMEM((MOE_BLOCK, D_MODEL), F32),
                pltpu.SemaphoreType.DMA(()),
                pltpu.SemaphoreType.DMA(()),
            ],
        ),
        out_shape=jax.ShapeDtypeStruct((n_slots, D_MODEL), F32),
        compiler_params=_cparams("arbitrary"),
        name="dispatch",
    )(pos1, pos2, padlo, padhi, n_valid, h2)


def _experts_kernel(be_ref, nv_ref, xs_ref, wg_ref, wu_ref, wd_ref, ys_ref, wg_bf, wu_bf, wd_bf):
    i = pl.program_id(0)
    live = i < nv_ref[0]

    new_expert = jnp.logical_or(i == 0, be_ref[i] != be_ref[jnp.maximum(i - 1, 0)])

    @pl.when(jnp.logical_and(live, new_expert))
    def _():
        wg_bf[...] = wg_ref[0, 0].astype(BF16)
        wu_bf[...] = wu_ref[0, 0].astype(BF16)
        wd_bf[...] = wd_ref[0, 0].astype(BF16)

    @pl.when(live)
    def _():
        x = xs_ref[...].astype(BF16)
        g = jnp.dot(x, wg_bf[...], preferred_element_type=F32)
        u = jnp.dot(x, wu_bf[...], preferred_element_type=F32)
        a = (g * jax.nn.sigmoid(g) * u).astype(BF16)
        ys_ref[...] = jnp.dot(a, wd_bf[...], preferred_element_type=F32)

    @pl.when(jnp.logical_not(live))
    def _():
        ys_ref[...] = jnp.zeros_like(ys_ref)


def _experts(block_e, n_valid, xs, wg, wu, wd, layer):
    n_slots = xs.shape[0]
    n_blocks = n_slots // MOE_BLOCK
    weight = lambda i, be, nv: (layer, be[i], 0, 0)
    return pl.pallas_call(
        _experts_kernel,
        grid_spec=pltpu.PrefetchScalarGridSpec(
            num_scalar_prefetch=2,
            grid=(n_blocks,),
            in_specs=[
                pl.BlockSpec((MOE_BLOCK, D_MODEL), lambda i, be, nv: (jnp.minimum(i, nv[0] - 1), 0)),
                pl.BlockSpec((1, 1, D_MODEL, D_EXPERT), weight),
                pl.BlockSpec((1, 1, D_MODEL, D_EXPERT), weight),
                pl.BlockSpec((1, 1, D_EXPERT, D_MODEL), weight),
            ],
            out_specs=pl.BlockSpec((MOE_BLOCK, D_MODEL), lambda i, be, nv: (i, 0)),
            scratch_shapes=[
                pltpu.VMEM((D_MODEL, D_EXPERT), BF16),
                pltpu.VMEM((D_MODEL, D_EXPERT), BF16),
                pltpu.VMEM((D_EXPERT, D_MODEL), BF16),
            ],
        ),
        out_shape=jax.ShapeDtypeStruct((n_slots, D_MODEL), F32),
        compiler_params=_cparams("arbitrary"),
        name="experts",
    )(block_e, n_valid, xs, wg, wu, wd)


def _combine_kernel(pos1_ref, pos2_ref, ys_ref, x1_ref, route_ref, out_ref, buf_ref, sems):
    tm = ROW_TILE
    i = pl.program_id(0)
    n_steps = pl.num_programs(0)

    def gather(step, slot, action):
        base = step * tm

        def body(r, c):
            for which, pos_ref in enumerate((pos1_ref, pos2_ref)):
                cp = pltpu.make_async_copy(
                    ys_ref.at[pl.ds(pos_ref[base + r], 1)],
                    buf_ref.at[slot, which, pl.ds(r, 1)],
                    sems.at[slot])
                if action == "start":
                    cp.start(priority=which)
                else:
                    cp.wait()
            return c
        lax.fori_loop(0, tm, body, 0, unroll=DMA_UNROLL)

    slot = lax.rem(i, 2)

    @pl.when(i == 0)
    def _():
        gather(0, 0, "start")

    @pl.when(i + 1 < n_steps)
    def _():
        gather(i + 1, 1 - slot, "start")

    gather(i, slot, "wait")
    col = lax.broadcasted_iota(jnp.int32, (tm, LANES), 1)
    route = route_ref[...]
    g1 = jnp.sum(jnp.where(col == ROUTE_G1, route, 0.0), axis=-1, keepdims=True)
    g2 = jnp.sum(jnp.where(col == ROUTE_G2, route, 0.0), axis=-1, keepdims=True)
    out_ref[...] = x1_ref[...] + (buf_ref[slot, 0] * g1 + buf_ref[slot, 1] * g2)


def _combine(pos1, pos2, ys, x1, route):
    n = x1.shape[0]
    tm = ROW_TILE
    return pl.pallas_call(
        _combine_kernel,
        grid_spec=pltpu.PrefetchScalarGridSpec(
            num_scalar_prefetch=2,
            grid=(n // tm,),
            in_specs=[
                pl.BlockSpec(memory_space=pl.ANY),
                pl.BlockSpec((tm, D_MODEL), lambda i, *_: (i, 0)),
                pl.BlockSpec((tm, LANES), lambda i, *_: (i, 0)),
            ],
            out_specs=pl.BlockSpec((tm, D_MODEL), lambda i, *_: (i, 0)),
            scratch_shapes=[
                pltpu.VMEM((2, 2, tm, D_MODEL), F32),
                pltpu.SemaphoreType.DMA((2,)),
            ],
        ),
        out_shape=jax.ShapeDtypeStruct((n, D_MODEL), F32),
        compiler_params=_cparams("arbitrary"),
        name="combine",
    )(pos1, pos2, ys, x1, route)


def _slot_plan(route_t, counts_row, n_tok):
    e1 = route_t[ROUTE_E1].astype(jnp.int32)
    e2 = route_t[ROUTE_E2].astype(jnp.int32)
    r1 = route_t[ROUTE_R1].astype(jnp.int32)
    r2 = route_t[ROUTE_R2].astype(jnp.int32)
    experts = jnp.arange(N_EXPERTS, dtype=jnp.int32)
    counts = counts_row[0, ROUTER_FINE_COL0:ROUTER_FINE_COL0 + N_EXPERTS].astype(jnp.int32)
    padded = (counts + MOE_BLOCK - 1) // MOE_BLOCK * MOE_BLOCK
    padded_ends = jnp.cumsum(padded)
    padded_starts = padded_ends - padded

    def start_of(e):
        return jnp.sum(jnp.where(e[None, :] == experts[:, None], padded_starts[:, None], 0), axis=0)

    pos1 = start_of(e1) + r1
    pos2 = start_of(e2) + r2
    n_blocks = -(-(2 * n_tok) // MOE_BLOCK) + N_EXPERTS
    n_valid = (padded_ends[-1] // MOE_BLOCK).astype(jnp.int32)
    block_start = jnp.arange(n_blocks, dtype=jnp.int32) * MOE_BLOCK
    block_start = jnp.minimum(block_start, padded_ends[-1] - MOE_BLOCK)
    block_e = jnp.sum((padded_ends[None, :] <= block_start[:, None]).astype(jnp.int32), axis=1)
    block_e = jnp.minimum(block_e, N_EXPERTS - 1)
    return dict(pos1=pos1, pos2=pos2, padlo=(padded_starts + counts).astype(jnp.int32),
                padhi=padded_ends.astype(jnp.int32), block_e=block_e,
                n_valid=n_valid.reshape(1), n_slots=n_blocks * MOE_BLOCK)


def _block_diag(w):
    nb, r, c = w.shape
    eye = jnp.eye(nb, dtype=w.dtype)
    return (w[:, :, None, :] * eye[:, None, :, None]).reshape(nb * r, nb * c)


def kernel(x, norm1_g, w_in, q_norm_g, k_norm_g, lru_conv_w, lru_conv_b, lru_wa, lru_ba, lru_wx, lru_bx, lru_lambda, conf_dw_w, conf_dw_b, conf_ln_g, conf_ln_b, conf_pw_w, conf_pw_b, out_norm_g, w_out, norm2_g, router_coarse, router_fine, exp_w_gate, exp_w_up, exp_w_down):
    bsz, seq, d = x.shape
    n = bsz * seq
    depth = w_in.shape[0]
    assert d == D_MODEL and n % DENSE_TILE == 0 and n % ROW_TILE == 0
    assert seq % MIX_TILE == 0 and seq % (ATTN_TILES * ATTN_BQ) == 0

    heads = D_SB // HEAD_DIM
    head_of = jnp.arange(D_SB, dtype=jnp.int32) // HEAD_DIM
    gsum = (head_of[:, None] == head_of[None, :]).astype(BF16)
    t_io = jnp.arange(DENSE_TILE, dtype=jnp.int32)
    ltri = (t_io[None, :] < t_io[:, None]).astype(BF16)
    row = lambda v: v.reshape(1, -1)

    xf = x.reshape(n, d)
    for l in range(depth):
        qkv, rest = _inproj(xf, row(norm1_g[l]), w_in[l].astype(BF16),
                            row(jnp.tile(q_norm_g[l], heads)), row(jnp.tile(k_norm_g[l], heads)), gsum)
        ya = _attention(qkv, bsz, seq).reshape(n, D_SB)
        wgates = jnp.concatenate([_block_diag(lru_wa[l]), _block_diag(lru_wx[l])], axis=1).astype(BF16)
        ybc = _mixers(rest, bsz, seq, lru_conv_w[l], row(lru_conv_b[l]), wgates,
                      row(lru_ba[l]), row(lru_bx[l]), row(lru_lambda[l]),
                      conf_dw_w[l], row(conf_dw_b[l]), row(conf_ln_g[l]), row(conf_ln_b[l]),
                      conf_pw_w[l].astype(BF16), row(conf_pw_b[l])).reshape(n, D_LRU + D_CONF)
        wr = jnp.concatenate([router_coarse[l], router_fine[l]], axis=1)
        wr = jnp.pad(wr, ((0, 0), (0, LANES - wr.shape[1]))).astype(BF16)
        x1, h2, route, route_t, counts = _outproj(ya, ybc, xf, row(out_norm_g[l]), w_out[l].astype(BF16),
                                                  row(norm2_g[l]), wr, ltri)
        plan = _slot_plan(route_t, counts, n)
        xs = _dispatch(plan["pos1"], plan["pos2"], plan["padlo"], plan["padhi"], plan["n_valid"], h2,
                       plan["n_slots"])
        ys = _experts(plan["block_e"], plan["n_valid"], xs, exp_w_gate, exp_w_up, exp_w_down, l)
        xf = _combine(plan["pos1"], plan["pos2"], ys, x1, route)
    return xf.reshape(bsz, seq, d)
```

```python
import functools

import jax
import jax.numpy as jnp
from jax import lax
from jax.experimental import pallas as pl
from jax.experimental.pallas import tpu as pltpu

F32 = jnp.float32
BF16 = jnp.bfloat16

D_MODEL = 1024
HEAD_DIM = 64
D_SB = 512
D_LRU = 256
LRU_BLOCKS = 4
LRU_CONV_WIDTH = 4
LRU_C = 8.0
D_CONF = 256
CONF_CONV_WIDTH = 31
D_QKV = 3 * D_SB
D_REST = 2 * D_LRU + 2 * D_CONF
N_GROUPS = 4
EXPERTS_PER_GROUP = 8
N_EXPERTS = N_GROUPS * EXPERTS_PER_GROUP
D_EXPERT = 512
D_PACKED = D_MODEL // 2
MOE_BLOCK = 256
EPS = 1e-6

LANES = 128
ROW_TILE = 512
DENSE_TILE = 512
ATTN_BQ = 256
ATTN_BK = 128
ATTN_TILES = 4
MIX_TILE = 512
CONF_HALO = 32
DMA_UNROLL = 8
LRU_HALO = 8
ATTN_DEAD_LOG = -110.0
VMEM_LIMIT = 48 * 1024 * 1024


def _cparams(*sem):
    return pltpu.CompilerParams(dimension_semantics=sem, vmem_limit_bytes=VMEM_LIMIT)


def _rms(y):
    return y * lax.rsqrt(jnp.mean(y * y, axis=-1, keepdims=True) + EPS)


def _pack_bf16_pairs(x):
    c = x.shape[1] // 2
    bits = pltpu.bitcast(x.astype(F32), jnp.uint32)
    return (bits[:, :c] >> 16) | (bits[:, c:] & jnp.uint32(0xFFFF0000))


def _unpack_bf16_pairs(words):
    low = pltpu.bitcast(words << 16, F32)
    high = pltpu.bitcast(words & jnp.uint32(0xFFFF0000), F32)
    return jnp.concatenate([low, high], axis=1).astype(BF16)


def _inproj_kernel(x_ref, g_ref, w_ref, qg_ref, kg_ref, gsum_ref, qkv_ref, rest_ref):
    h = (_rms(x_ref[...]) * g_ref[...]).astype(BF16)

    def head_norm(p, gain):
        s = jnp.dot((p * p).astype(BF16), gsum_ref[...], preferred_element_type=F32)
        return p * lax.rsqrt(s * (1.0 / HEAD_DIM) + EPS) * gain

    q = jnp.dot(h, w_ref[:, 0:D_SB], preferred_element_type=F32)
    qkv_ref[:, 0:D_SB] = head_norm(q, qg_ref[...]).astype(BF16)
    k = jnp.dot(h, w_ref[:, D_SB:2 * D_SB], preferred_element_type=F32)
    qkv_ref[:, D_SB:2 * D_SB] = head_norm(k, kg_ref[...]).astype(BF16)
    v = jnp.dot(h, w_ref[:, 2 * D_SB:D_QKV], preferred_element_type=F32)
    qkv_ref[:, 2 * D_SB:D_QKV] = v.astype(BF16)
    rest_ref[...] = jnp.dot(h, w_ref[:, D_QKV:], preferred_element_type=F32)


def _inproj(x, g, w, qg, kg, gsum):
    n = x.shape[0]
    tm = DENSE_TILE
    const = lambda i: (0, 0)
    return pl.pallas_call(
        _inproj_kernel,
        grid=(n // tm,),
        in_specs=[
            pl.BlockSpec((tm, D_MODEL), lambda i: (i, 0)),
            pl.BlockSpec((1, D_MODEL), const),
            pl.BlockSpec((D_MODEL, D_QKV + D_REST), const),
            pl.BlockSpec((1, D_SB), const),
            pl.BlockSpec((1, D_SB), const),
            pl.BlockSpec((D_SB, D_SB), const),
        ],
        out_specs=[
            pl.BlockSpec((tm, D_QKV), lambda i: (i, 0)),
            pl.BlockSpec((tm, D_REST), lambda i: (i, 0)),
        ],
        out_shape=[
            jax.ShapeDtypeStruct((n, D_QKV), BF16),
            jax.ShapeDtypeStruct((n, D_REST), F32),
        ],
        compiler_params=_cparams("arbitrary"),
        name="inproj",
    )(x, g, w, qg, kg, gsum)


def _attn_kernel(q_ref, k_ref, v_ref, o_ref, *scratch):
    def tile(tt, carry):
        rows = pl.ds(pl.multiple_of(tt * ATTN_BQ, ATTN_BQ), ATTN_BQ)
        _attn_tile(pl.program_id(2) * ATTN_TILES + tt, q_ref.at[0, rows], k_ref, v_ref, o_ref.at[0, rows], *scratch)
        return carry
    lax.fori_loop(0, ATTN_TILES, tile, 0)


def _attn_tile(i, q_ref, k_ref, v_ref, o_ref, z0, z1, lbc0, lbc1, hl0, hl1, c0, c1, cm0, cm1, w0, w1, acc_ref):
    bq, bk = ATTN_BQ, ATTN_BK
    heads = (0, 1)
    z_refs, lbc_refs, hilo_refs, c_refs, cmax_refs = (z0, z1), (lbc0, lbc1), (hl0, hl1), (c0, c1), (cm0, cm1)
    w_refs = (w0, w1)
    q = q_ref[...] * jnp.asarray(HEAD_DIM ** -0.5, BF16)
    first = lax.broadcasted_iota(jnp.int32, (bk, LANES), 1) < HEAD_DIM
    t_pos = lax.broadcasted_iota(jnp.int32, (bq, bk), 0) + i * bq
    s_io = lax.broadcasted_iota(jnp.int32, (bq, bk), 1)
    earlier = (lax.broadcasted_iota(jnp.int32, (bk, bk), 0)
               > lax.broadcasted_iota(jnp.int32, (bk, bk), 1))
    neg_later = jnp.where(earlier, -1.0, 0.0).astype(BF16)
    neg_later2 = jnp.concatenate([neg_later, neg_later], axis=0)

    def rows(m):
        return pl.ds(pl.multiple_of(jnp.maximum(m, 0) * bk, bk), bk)

    def causal(m):
        return (s_io + m * bk) < t_pos

    def per_head(x):
        zero = jnp.zeros_like(x)
        return jnp.concatenate([jnp.where(first, x, zero), jnp.where(first, zero, x)], axis=0)

    def stage0(m, s):
        z = lax.dot_general(q, per_head(k_ref[0, rows(m), :]), (((1,), (1,)), ((), ())),
                            preferred_element_type=F32)
        for h in heads:
            z_refs[s][h] = z[:, h * bk:(h + 1) * bk]

    def stage1(m, s, masked):
        mask = causal(m) if masked else None
        for h in heads:
            z = z_refs[s][h]
            c = c_refs[s][h]
            drop = jnp.maximum(z, 0.0) + jnp.log(1.0 + jnp.exp(-jnp.abs(z)))
            lbc_refs[s][h] = (z - drop) + c
            if masked:
                drop = jnp.where(mask, drop, 0.0)
            hi = drop.astype(BF16)
            lo = (drop - hi.astype(F32)).astype(BF16)
            hilo_refs[s][h] = jnp.concatenate([hi, lo], axis=1)
            c_next = c - jnp.sum(drop, axis=-1, keepdims=True)
            c_refs[1 - s][h] = c_next
            cmax_refs[1 - s][h] = jnp.max(c_next.reshape(bq // 8, 8, LANES), axis=0)

    def stage2_sums(s):
        return [jnp.dot(hilo_refs[s][h], neg_later2, preferred_element_type=F32) for h in heads]

    def stage2_apply(m, s, within, masked):
        mask = causal(m) if masked else None
        ws = []
        for h in heads:
            w = jnp.exp(lbc_refs[s][h] + within[h])
            if masked:
                w = jnp.where(mask, w, 0.0)
            ws.append(w.astype(BF16))
        w_refs[s][...] = jnp.concatenate(ws, axis=1)

    def stage3(m, s):
        acc_ref[...] += jnp.dot(w_refs[s][...], per_head(v_ref[0, rows(m), :]), preferred_element_type=F32)

    def alive(m, s):
        return jnp.logical_and(m >= 0, jnp.max(cmax_refs[s][...]) > ATTN_DEAD_LOG)

    top = (i + 1) * (bq // bk) - 1
    acc_ref[...] = jnp.zeros_like(acc_ref)
    c_refs[1][...] = jnp.zeros((2, bq, LANES), F32)

    def step(m, s, masked_next=False, masked=False, has_prev=True):
        live_next = alive(m - 1, 1 - s)
        if has_prev:
            stage3(m + 1, 1 - s)
        within = stage2_sums(s)
        stage0(m - 2, s)
        stage1(m - 1, 1 - s, masked_next)
        stage2_apply(m, s, within, masked)
        return live_next

    stage0(top, 1)
    stage0(top - 1, 0)
    stage1(top, 1, True)
    step(top, 1, masked_next=True, masked=True, has_prev=False)
    step(top - 1, 0, masked=True)

    def body(st):
        m, _ = st
        live_next = lax.cond((m & 1) == 0, functools.partial(step, m, 0), functools.partial(step, m, 1))
        return m - 1, live_next

    m_end, _ = lax.while_loop(lambda st: st[1], body, (top - 2, alive(top - 2, 1)))
    for s in (0, 1):
        pl.when(((m_end + 1) & 1) == s)(functools.partial(stage3, m_end + 1, s))
    o_ref[...] = acc_ref[...]


def _attention(qkv, bsz, seq):
    bq, bk = ATTN_BQ, ATTN_BK
    assert bq == 2 * bk
    pairs = D_SB // LANES
    qkv3 = qkv.reshape(bsz, seq, D_QKV)
    step_rows = ATTN_TILES * bq
    return pl.pallas_call(
        _attn_kernel,
        grid=(bsz, pairs, seq // step_rows),
        in_specs=[
            pl.BlockSpec((1, step_rows, LANES), lambda b, p, i: (b, i, p)),
            pl.BlockSpec((1, seq, LANES), lambda b, p, i: (b, 0, pairs + p)),
            pl.BlockSpec((1, seq, LANES), lambda b, p, i: (b, 0, 2 * pairs + p)),
        ],
        out_specs=pl.BlockSpec((1, step_rows, LANES), lambda b, p, i: (b, i, p)),
        out_shape=jax.ShapeDtypeStruct((bsz, seq, D_SB), F32),
        scratch_shapes=(
            [pltpu.VMEM((2, bq, bk), F32)] * 2
            + [pltpu.VMEM((2, bq, bk), F32)] * 2
            + [pltpu.VMEM((2, bq, 2 * bk), BF16)] * 2
            + [pltpu.VMEM((2, bq, LANES), F32)] * 2
            + [pltpu.VMEM((2, 8, LANES), F32)] * 2
            + [pltpu.VMEM((bq, 2 * bk), BF16)] * 2
            + [pltpu.VMEM((bq, LANES), F32)]
        ),
        compiler_params=_cparams("arbitrary", "arbitrary", "arbitrary"),
        name="attn",
    )(qkv3, qkv3, qkv3)


def _shift_rows(x, d, fill):
    t = x.shape[0]
    if d % 8 == 0:
        return jnp.concatenate([jnp.full((d, x.shape[1]), fill, x.dtype), x[:t - d]], axis=0)
    rows = lax.broadcasted_iota(jnp.int32, x.shape, 0)
    return jnp.where(rows < d, fill, pltpu.roll(x, d, axis=0))


def _mixers_kernel(rest_ref, lcw_ref, lcb_ref, wg_ref, ba_ref, bx_ref, lam_ref,
                   cw_ref, cb_ref, lng_ref, lnb_ref, pw_ref, pb_ref,
                   y_ref, ltail_ref, h_ref, ctail_ref, lwin_ref, cwin_ref):
    t = MIX_TILE

    @pl.when(pl.program_id(1) == 0)
    def _():
        ltail_ref[...] = jnp.zeros_like(ltail_ref)
        h_ref[...] = jnp.zeros_like(h_ref)
        ctail_ref[...] = jnp.zeros_like(ctail_ref)

    xr = rest_ref[0, :, 0:D_LRU]
    xg = rest_ref[0, :, D_LRU:2 * D_LRU]
    val = rest_ref[0, :, 2 * D_LRU:2 * D_LRU + D_CONF]
    gate = rest_ref[0, :, 2 * D_LRU + D_CONF:]

    lwin_ref[0:LRU_HALO, :] = ltail_ref[...]
    lwin_ref[LRU_HALO:, :] = xr
    ltail_ref[...] = xr[t - LRU_HALO:, :]
    conv = lcb_ref[...]
    for k in range(LRU_CONV_WIDTH):
        off = LRU_HALO - (LRU_CONV_WIDTH - 1) + k
        conv = conv + lcw_ref[k:k + 1, :] * lwin_ref[off:off + t, :]
    gates = jnp.dot(conv.astype(BF16), wg_ref[...], preferred_element_type=F32)
    gate_r = gates[:, 0:D_LRU] + ba_ref[...]
    gate_i = gates[:, D_LRU:] + bx_ref[...]
    log_a = LRU_C * jax.nn.sigmoid(gate_r) * jax.nn.log_sigmoid(lam_ref[...])
    a = jnp.exp(log_a)
    th = jnp.tanh(log_a)
    u = jnp.sqrt(-2.0 * th / (1.0 - th)) * (jax.nn.sigmoid(gate_i) * conv)
    d = 1
    while d < t:
        u = a * _shift_rows(u, d, 0.0) + u
        a = a * _shift_rows(a, d, 1.0)
        d *= 2
    h = a * h_ref[...] + u
    h_ref[...] = h[t - 1:t, :]
    y_ref[0, :, 0:D_LRU] = h * jax.nn.gelu(xg)

    glu = val * jax.nn.sigmoid(gate)
    cwin_ref[0:CONF_HALO, :] = ctail_ref[...]
    cwin_ref[CONF_HALO:, :] = glu
    ctail_ref[...] = glu[t - CONF_HALO:, :]
    acc = cb_ref[...]
    first_off = CONF_HALO - (CONF_CONV_WIDTH - 1)
    for r in range(8):
        offs = [o for o in range(first_off, first_off + CONF_CONV_WIDTH) if o % 8 == r]
        span = t + (8 if r else 0)
        part = None
        for o in offs:
            term = cw_ref[o - first_off:o - first_off + 1, :] * cwin_ref[o - r:o - r + span, :]
            part = term if part is None else part + term
        acc = acc + part[r:r + t, :]
    mu = jnp.mean(acc, axis=-1, keepdims=True)
    cen = acc - mu
    var = jnp.mean(cen * cen, axis=-1, keepdims=True)
    ln = cen * lax.rsqrt(var + EPS) * lng_ref[...] + lnb_ref[...]
    sw = (ln * jax.nn.sigmoid(ln)).astype(BF16)
    y_ref[0, :, D_LRU:] = jnp.dot(sw, pw_ref[...], preferred_element_type=F32) + pb_ref[...]


def _mixers(rest, bsz, seq, lcw, lcb, wgates, ba, bx, lam, cw, cb, lng, lnb, pw, pb):
    t = MIX_TILE
    rest3 = rest.reshape(bsz, seq, D_REST)
    const = lambda b, i: (0, 0)
    row = lambda d: pl.BlockSpec((1, d), const)
    return pl.pallas_call(
        _mixers_kernel,
        grid=(bsz, seq // t),
        in_specs=[
            pl.BlockSpec((1, t, D_REST), lambda b, i: (b, i, 0)),
            pl.BlockSpec((LRU_CONV_WIDTH, D_LRU), const), row(D_LRU),
            pl.BlockSpec((D_LRU, 2 * D_LRU), const), row(D_LRU), row(D_LRU), row(D_LRU),
            pl.BlockSpec((CONF_CONV_WIDTH, D_CONF), const), row(D_CONF),
            row(D_CONF), row(D_CONF),
            pl.BlockSpec((D_CONF, D_CONF), const), row(D_CONF),
        ],
        out_specs=pl.BlockSpec((1, t, D_LRU + D_CONF), lambda b, i: (b, i, 0)),
        out_shape=jax.ShapeDtypeStruct((bsz, seq, D_LRU + D_CONF), F32),
        scratch_shapes=[
            pltpu.VMEM((LRU_HALO, D_LRU), F32),
            pltpu.VMEM((1, D_LRU), F32),
            pltpu.VMEM((CONF_HALO, D_CONF), F32),
            pltpu.VMEM((LRU_HALO + t, D_LRU), F32),
            pltpu.VMEM((CONF_HALO + t, D_CONF), F32),
        ],
        compiler_params=_cparams("arbitrary", "arbitrary"),
        name="mixers",
    )(rest3, lcw, lcb, wgates, ba, bx, lam, cw, cb, lng, lnb, pw, pb)


ROUTE_G1, ROUTE_G2, ROUTE_E1, ROUTE_E2, ROUTE_R1, ROUTE_R2 = range(6)
ROUTE_ROWS = 8
ROUTER_FINE_COL0 = N_GROUPS


def _outproj_kernel(ya_ref, ybc_ref, x_ref, gout_ref, wout_ref, g2_ref, wr_ref, ltri_ref,
                    x1_ref, h2_ref, route_ref, route_t_ref, counts_ref, cnt_ref):
    @pl.when(pl.program_id(0) == 0)
    def _():
        cnt_ref[...] = jnp.zeros_like(cnt_ref)

    y = jnp.concatenate(
        [_rms(ya_ref[...]), _rms(ybc_ref[:, 0:D_LRU]), _rms(ybc_ref[:, D_LRU:])], axis=1)
    y = (y * gout_ref[...]).astype(BF16)
    x1 = x_ref[...] + jnp.dot(y, wout_ref[...], preferred_element_type=F32)
    x1_ref[...] = x1
    h2 = (_rms(x1) * g2_ref[...]).astype(BF16)
    h2_ref[...] = _pack_bf16_pairs(h2)
    lf = jnp.dot(h2, wr_ref[...], preferred_element_type=F32)

    tm = lf.shape[0]
    col_i = lax.broadcasted_iota(jnp.int32, (tm, LANES), 1)
    col = col_i.astype(F32)
    neg = -jnp.inf
    big = float(LANES)
    is_c = col_i < N_GROUPS
    lc = jnp.where(is_c, lf, neg)
    mc = jnp.max(lc, axis=-1, keepdims=True)
    grp = jnp.min(jnp.where(lc == mc, col, big), axis=-1, keepdims=True)
    w_grp = 1.0 / jnp.sum(jnp.where(is_c, jnp.exp(lf - mc), 0.0), axis=-1, keepdims=True)
    lo = ROUTER_FINE_COL0 + EXPERTS_PER_GROUP * grp
    in_grp = jnp.logical_and(col >= lo, col < lo + EXPERTS_PER_GROUP)
    l1 = jnp.where(in_grp, lf, neg)
    v1 = jnp.max(l1, axis=-1, keepdims=True)
    i1 = jnp.min(jnp.where(l1 == v1, col, big), axis=-1, keepdims=True)
    sel1 = col == i1
    l2 = jnp.where(sel1, neg, l1)
    v2 = jnp.max(l2, axis=-1, keepdims=True)
    i2 = jnp.min(jnp.where(l2 == v2, col, big), axis=-1, keepdims=True)
    sel2 = col == i2
    e21 = jnp.exp(v2 - v1)
    g1 = w_grp / (1.0 + e21)
    g2 = w_grp * e21 / (1.0 + e21)

    oh1 = jnp.where(sel1, 1.0, 0.0)
    oh2 = jnp.where(sel2, 1.0, 0.0)
    before1 = jnp.dot(ltri_ref[...], oh1.astype(BF16), preferred_element_type=F32)
    before2 = jnp.dot(ltri_ref[...], oh2.astype(BF16), preferred_element_type=F32)
    cnt = cnt_ref[...]
    tot1 = jnp.sum(oh1, axis=0, keepdims=True)
    tot2 = jnp.sum(oh2, axis=0, keepdims=True)
    r1 = jnp.sum(jnp.where(sel1, before1 + cnt, 0.0), axis=-1, keepdims=True)
    r2 = jnp.sum(jnp.where(sel2, before2 + cnt + tot1, 0.0), axis=-1, keepdims=True)
    cnt_ref[...] = cnt + tot1 + tot2

    out = jnp.zeros((tm, LANES), F32)
    for lane_id, val in ((ROUTE_G1, g1), (ROUTE_G2, g2),
                         (ROUTE_E1, i1 - ROUTER_FINE_COL0), (ROUTE_E2, i2 - ROUTER_FINE_COL0),
                         (ROUTE_R1, r1), (ROUTE_R2, r2)):
        out = jnp.where(col_i == lane_id, val, out)
    route_ref[...] = out
    route_t_ref[...] = out.T[0:ROUTE_ROWS, :]
    counts_ref[...] = cnt_ref[...]


def _outproj(ya, ybc, x, gout, wout, g2, wr, ltri):
    n = x.shape[0]
    tm = DENSE_TILE
    const = lambda i: (0, 0)
    tile = lambda d: pl.BlockSpec((tm, d), lambda i: (i, 0))
    return pl.pallas_call(
        _outproj_kernel,
        grid=(n // tm,),
        in_specs=[
            tile(D_SB), tile(D_LRU + D_CONF), tile(D_MODEL),
            pl.BlockSpec((1, D_MODEL), const),
            pl.BlockSpec((D_MODEL, D_MODEL), const),
            pl.BlockSpec((1, D_MODEL), const),
            pl.BlockSpec((D_MODEL, LANES), const),
            pl.BlockSpec((tm, tm), const),
        ],
        out_specs=[tile(D_MODEL), tile(D_PACKED), tile(LANES),
                   pl.BlockSpec((ROUTE_ROWS, tm), lambda i: (0, i)),
                   pl.BlockSpec((1, LANES), const)],
        out_shape=[
            jax.ShapeDtypeStruct((n, D_MODEL), F32),
            jax.ShapeDtypeStruct((n, D_PACKED), jnp.uint32),
            jax.ShapeDtypeStruct((n, LANES), F32),
            jax.ShapeDtypeStruct((ROUTE_ROWS, n), F32),
            jax.ShapeDtypeStruct((1, LANES), F32),
        ],
        scratch_shapes=[pltpu.VMEM((1, LANES), F32)],
        compiler_params=_cparams("arbitrary"),
        name="outproj",
    )(ya, ybc, x, gout, wout, g2, wr, ltri)


def _row_copy(src_ref, src_row, dst_ref, dst_row, sem):
    return pltpu.make_async_copy(src_ref.at[pl.ds(src_row, 1)], dst_ref.at[pl.ds(dst_row, 1)], sem)


def _dispatch_kernel(pos1_ref, pos2_ref, padlo_ref, padhi_ref, nv_ref, h2_ref, xs_ref, zero_ref, sem, zsem):
    tm = ROW_TILE
    i = pl.program_id(0)
    base = i * tm

    @pl.when(i == 0)
    def _():
        zero_ref[...] = jnp.zeros_like(zero_ref)

        def per_expert(e, _):
            def start(r, c):
                _row_copy(zero_ref, 0, xs_ref, r, zsem).start()
                return c
            lax.fori_loop(padlo_ref[e], padhi_ref[e], start, 0)

            def wait(r, c):
                _row_copy(zero_ref, 0, xs_ref, r, zsem).wait()
                return c
            lax.fori_loop(padlo_ref[e], padhi_ref[e], wait, 0)
            return 0
        lax.fori_loop(0, N_EXPERTS, per_expert, 0)

        def idle_copy(b):
            rows = pl.ds(pl.multiple_of(b * MOE_BLOCK, MOE_BLOCK), MOE_BLOCK)
            return pltpu.make_async_copy(zero_ref, xs_ref.at[rows], zsem)

        def idle_start(b, c):
            idle_copy(b).start()
            return c

        def idle_wait(b, c):
            idle_copy(b).wait()
            return c
        n_blocks = xs_ref.shape[0] // MOE_BLOCK
        lax.fori_loop(nv_ref[0], n_blocks, idle_start, 0)
        lax.fori_loop(nv_ref[0], n_blocks, idle_wait, 0)

    def start(r, c):
        _row_copy(h2_ref, r, xs_ref, pos1_ref[base + r], sem).start()
        _row_copy(h2_ref, r, xs_ref, pos2_ref[base + r], sem).start(priority=1)
        return c
    lax.fori_loop(0, tm, start, 0, unroll=DMA_UNROLL)

    def wait(r, c):
        _row_copy(h2_ref, r, xs_ref, pos1_ref[base + r], sem).wait()
        _row_copy(h2_ref, r, xs_ref, pos2_ref[base + r], sem).wait()
        return c
    lax.fori_loop(0, tm, wait, 0, unroll=DMA_UNROLL)


def _dispatch(pos1, pos2, padlo, padhi, n_valid, h2, n_slots):
    n = h2.shape[0]
    tm = ROW_TILE
    return pl.pallas_call(
        _dispatch_kernel,
        grid_spec=pltpu.PrefetchScalarGridSpec(
            num_scalar_prefetch=5,
            grid=(n // tm,),
            in_specs=[pl.BlockSpec((tm, D_PACKED), lambda i, *_: (i, 0))],
            out_specs=pl.BlockSpec(memory_space=pl.ANY),
            scratch_shapes=[
                pltpu.VMEM((MOE_BLOCK, D_PACKED), jnp.uint32),
                pltpu.SemaphoreType.DMA(()),
                pltpu.SemaphoreType.DMA(()),
            ],
        ),
        out_shape=jax.ShapeDtypeStruct((n_slots, D_PACKED), jnp.uint32),
        compiler_params=_cparams("arbitrary"),
        name="dispatch",
    )(pos1, pos2, padlo, padhi, n_valid, h2)


def _experts_kernel(be_ref, nv_ref, xs_ref, wg_ref, wu_ref, wd_ref, ys_ref, wg_bf, wu_bf, wd_bf):
    i = pl.program_id(0)
    live = i < nv_ref[0]

    new_expert = jnp.logical_or(i == 0, be_ref[i] != be_ref[jnp.maximum(i - 1, 0)])

    @pl.when(jnp.logical_and(live, new_expert))
    def _():
        wg_bf[...] = wg_ref[0, 0].astype(BF16)
        wu_bf[...] = wu_ref[0, 0].astype(BF16)
        wd_bf[...] = wd_ref[0, 0].astype(BF16)

    @pl.when(live)
    def _():
        x = _unpack_bf16_pairs(xs_ref[...])
        g = jnp.dot(x, wg_bf[...], preferred_element_type=F32)
        u = jnp.dot(x, wu_bf[...], preferred_element_type=F32)
        a = (g * jax.nn.sigmoid(g) * u).astype(BF16)
        ys_ref[...] = jnp.dot(a, wd_bf[...], preferred_element_type=F32)

    @pl.when(jnp.logical_not(live))
    def _():
        ys_ref[...] = jnp.zeros_like(ys_ref)


def _experts(block_e, n_valid, xs, wg, wu, wd, layer):
    n_slots = xs.shape[0]
    n_blocks = n_slots // MOE_BLOCK
    weight = lambda i, be, nv: (layer, be[i], 0, 0)
    return pl.pallas_call(
        _experts_kernel,
        grid_spec=pltpu.PrefetchScalarGridSpec(
            num_scalar_prefetch=2,
            grid=(n_blocks,),
            in_specs=[
                pl.BlockSpec((MOE_BLOCK, D_PACKED), lambda i, be, nv: (jnp.minimum(i, nv[0] - 1), 0)),
                pl.BlockSpec((1, 1, D_MODEL, D_EXPERT), weight),
                pl.BlockSpec((1, 1, D_MODEL, D_EXPERT), weight),
                pl.BlockSpec((1, 1, D_EXPERT, D_MODEL), weight),
            ],
            out_specs=pl.BlockSpec((MOE_BLOCK, D_MODEL), lambda i, be, nv: (i, 0)),
            scratch_shapes=[
                pltpu.VMEM((D_MODEL, D_EXPERT), BF16),
                pltpu.VMEM((D_MODEL, D_EXPERT), BF16),
                pltpu.VMEM((D_EXPERT, D_MODEL), BF16),
            ],
        ),
        out_shape=jax.ShapeDtypeStruct((n_slots, D_MODEL), F32),
        compiler_params=_cparams("arbitrary"),
        name="experts",
    )(block_e, n_valid, xs, wg, wu, wd)


def _combine_kernel(pos1_ref, pos2_ref, ys_ref, x1_ref, route_ref, out_ref, buf_ref, sems):
    tm = ROW_TILE
    i = pl.program_id(0)
    n_steps = pl.num_programs(0)

    def gather(step, slot, action):
        base = step * tm

        def body(r, c):
            for which, pos_ref in enumerate((pos1_ref, pos2_ref)):
                cp = pltpu.make_async_copy(
                    ys_ref.at[pl.ds(pos_ref[base + r], 1)],
                    buf_ref.at[slot, which, pl.ds(r, 1)],
                    sems.at[slot])
                if action == "start":
                    cp.start(priority=which)
                else:
                    cp.wait()
            return c
        lax.fori_loop(0, tm, body, 0, unroll=DMA_UNROLL)

    slot = lax.rem(i, 2)

    @pl.when(i == 0)
    def _():
        gather(0, 0, "start")

    @pl.when(i + 1 < n_steps)
    def _():
        gather(i + 1, 1 - slot, "start")

    gather(i, slot, "wait")
    col = lax.broadcasted_iota(jnp.int32, (tm, LANES), 1)
    route = route_ref[...]
    g1 = jnp.sum(jnp.where(col == ROUTE_G1, route, 0.0), axis=-1, keepdims=True)
    g2 = jnp.sum(jnp.where(col == ROUTE_G2, route, 0.0), axis=-1, keepdims=True)
    out_ref[...] = x1_ref[...] + (buf_ref[slot, 0] * g1 + buf_ref[slot, 1] * g2)


def _combine(pos1, pos2, ys, x1, route):
    n = x1.shape[0]
    tm = ROW_TILE
    return pl.pallas_call(
        _combine_kernel,
        grid_spec=pltpu.PrefetchScalarGridSpec(
            num_scalar_prefetch=2,
            grid=(n // tm,),
            in_specs=[
                pl.BlockSpec(memory_space=pl.ANY),
                pl.BlockSpec((tm, D_MODEL), lambda i, *_: (i, 0)),
                pl.BlockSpec((tm, LANES), lambda i, *_: (i, 0)),
            ],
            out_specs=pl.BlockSpec((tm, D_MODEL), lambda i, *_: (i, 0)),
            scratch_shapes=[
                pltpu.VMEM((2, 2, tm, D_MODEL), F32),
                pltpu.SemaphoreType.DMA((2,)),
            ],
        ),
        out_shape=jax.ShapeDtypeStruct((n, D_MODEL), F32),
        compiler_params=_cparams("arbitrary"),
        name="combine",
    )(pos1, pos2, ys, x1, route)


def _slot_plan(route_t, counts_row, n_tok):
    e1 = route_t[ROUTE_E1].astype(jnp.int32)
    e2 = route_t[ROUTE_E2].astype(jnp.int32)
    r1 = route_t[ROUTE_R1].astype(jnp.int32)
    r2 = route_t[ROUTE_R2].astype(jnp.int32)
    experts = jnp.arange(N_EXPERTS, dtype=jnp.int32)
    counts = counts_row[0, ROUTER_FINE_COL0:ROUTER_FINE_COL0 + N_EXPERTS].astype(jnp.int32)
    padded = (counts + MOE_BLOCK - 1) // MOE_BLOCK * MOE_BLOCK
    padded_ends = jnp.cumsum(padded)
    padded_starts = padded_ends - padded

    def start_of(e):
        return jnp.sum(jnp.where(e[None, :] == experts[:, None], padded_starts[:, None], 0), axis=0)

    pos1 = start_of(e1) + r1
    pos2 = start_of(e2) + r2
    n_blocks = -(-(2 * n_tok) // MOE_BLOCK) + N_EXPERTS
    n_valid = (padded_ends[-1] // MOE_BLOCK).astype(jnp.int32)
    block_start = jnp.arange(n_blocks, dtype=jnp.int32) * MOE_BLOCK
    block_start = jnp.minimum(block_start, padded_ends[-1] - MOE_BLOCK)
    block_e = jnp.sum((padded_ends[None, :] <= block_start[:, None]).astype(jnp.int32), axis=1)
    block_e = jnp.minimum(block_e, N_EXPERTS - 1)
    return dict(pos1=pos1, pos2=pos2, padlo=(padded_starts + counts).astype(jnp.int32),
                padhi=padded_ends.astype(jnp.int32), block_e=block_e,
                n_valid=n_valid.reshape(1), n_slots=n_blocks * MOE_BLOCK)


def _block_diag(w):
    nb, r, c = w.shape
    eye = jnp.eye(nb, dtype=w.dtype)
    return (w[:, :, None, :] * eye[:, None, :, None]).reshape(nb * r, nb * c)


def kernel(x, norm1_g, w_in, q_norm_g, k_norm_g, lru_conv_w, lru_conv_b, lru_wa, lru_ba, lru_wx, lru_bx, lru_lambda, conf_dw_w, conf_dw_b, conf_ln_g, conf_ln_b, conf_pw_w, conf_pw_b, out_norm_g, w_out, norm2_g, router_coarse, router_fine, exp_w_gate, exp_w_up, exp_w_down):
    bsz, seq, d = x.shape
    n = bsz * seq
    depth = w_in.shape[0]
    assert d == D_MODEL and n % DENSE_TILE == 0 and n % ROW_TILE == 0
    assert seq % MIX_TILE == 0 and seq % (ATTN_TILES * ATTN_BQ) == 0

    heads = D_SB // HEAD_DIM
    head_of = jnp.arange(D_SB, dtype=jnp.int32) // HEAD_DIM
    gsum = (head_of[:, None] == head_of[None, :]).astype(BF16)
    t_io = jnp.arange(DENSE_TILE, dtype=jnp.int32)
    ltri = (t_io[None, :] < t_io[:, None]).astype(BF16)
    row = lambda v: v.reshape(1, -1)

    xf = x.reshape(n, d)
    for l in range(depth):
        qkv, rest = _inproj(xf, row(norm1_g[l]), w_in[l].astype(BF16),
                            row(jnp.tile(q_norm_g[l], heads)), row(jnp.tile(k_norm_g[l], heads)), gsum)
        ya = _attention(qkv, bsz, seq).reshape(n, D_SB)
        wgates = jnp.concatenate([_block_diag(lru_wa[l]), _block_diag(lru_wx[l])], axis=1).astype(BF16)
        ybc = _mixers(rest, bsz, seq, lru_conv_w[l], row(lru_conv_b[l]), wgates,
                      row(lru_ba[l]), row(lru_bx[l]), row(lru_lambda[l]),
                      conf_dw_w[l], row(conf_dw_b[l]), row(conf_ln_g[l]), row(conf_ln_b[l]),
                      conf_pw_w[l].astype(BF16), row(conf_pw_b[l])).reshape(n, D_LRU + D_CONF)
        wr = jnp.concatenate([router_coarse[l], router_fine[l]], axis=1)
        wr = jnp.pad(wr, ((0, 0), (0, LANES - wr.shape[1]))).astype(BF16)
        x1, h2, route, route_t, counts = _outproj(ya, ybc, xf, row(out_norm_g[l]), w_out[l].astype(BF16),
                                                  row(norm2_g[l]), wr, ltri)
        plan = _slot_plan(route_t, counts, n)
        xs = _dispatch(plan["pos1"], plan["pos2"], plan["padlo"], plan["padhi"], plan["n_valid"], h2,
                       plan["n_slots"])
        ys = _experts(plan["block_e"], plan["n_valid"], xs, exp_w_gate, exp_w_up, exp_w_down, l)
        xf = _combine(plan["pos1"], plan["pos2"], ys, x1, route)
    return xf.reshape(bsz, seq, d)
```

```python
import functools

import jax
import jax.numpy as jnp
from jax import lax
from jax.experimental import pallas as pl
from jax.experimental.pallas import tpu as pltpu

F32 = jnp.float32
BF16 = jnp.bfloat16

D_MODEL = 1024
HEAD_DIM = 64
D_SB = 512
D_LRU = 256
LRU_BLOCKS = 4
LRU_CONV_WIDTH = 4
LRU_C = 8.0
D_CONF = 256
CONF_CONV_WIDTH = 31
D_QKV = 3 * D_SB
D_REST = 2 * D_LRU + 2 * D_CONF
N_GROUPS = 4
EXPERTS_PER_GROUP = 8
N_EXPERTS = N_GROUPS * EXPERTS_PER_GROUP
D_EXPERT = 512
D_PACKED = D_MODEL // 2
MOE_BLOCK = 256
EPS = 1e-6

LANES = 128
ROW_TILE = 512
DENSE_TILE = 512
ATTN_BQ = 256
ATTN_BK = 128
ATTN_TILES = 4
MIX_TILE = 512
CONF_HALO = 32
DMA_UNROLL = 8
LRU_HALO = 8
ATTN_DEAD_LOG = -110.0
VMEM_LIMIT = 48 * 1024 * 1024


def _cparams(*sem):
    return pltpu.CompilerParams(dimension_semantics=sem, vmem_limit_bytes=VMEM_LIMIT)


def _rms(y):
    return y * lax.rsqrt(jnp.mean(y * y, axis=-1, keepdims=True) + EPS)


def _pack_bf16_pairs(x):
    c = x.shape[1] // 2
    bits = pltpu.bitcast(x.astype(F32), jnp.uint32)
    return (bits[:, :c] >> 16) | (bits[:, c:] & jnp.uint32(0xFFFF0000))


def _unpack_bf16_pairs(words):
    low = pltpu.bitcast(words << 16, F32)
    high = pltpu.bitcast(words & jnp.uint32(0xFFFF0000), F32)
    return jnp.concatenate([low, high], axis=1).astype(BF16)


def _inproj_kernel(x_ref, g_ref, w_ref, qg_ref, kg_ref, gsum_ref, qkv_ref, rest_ref):
    h = (_rms(x_ref[...]) * g_ref[...]).astype(BF16)

    def head_norm(p, gain):
        s = jnp.dot((p * p).astype(BF16), gsum_ref[...], preferred_element_type=F32)
        return p * lax.rsqrt(s * (1.0 / HEAD_DIM) + EPS) * gain

    q = jnp.dot(h, w_ref[:, 0:D_SB], preferred_element_type=F32)
    qkv_ref[:, 0:D_SB] = head_norm(q, qg_ref[...]).astype(BF16)
    k = jnp.dot(h, w_ref[:, D_SB:2 * D_SB], preferred_element_type=F32)
    qkv_ref[:, D_SB:2 * D_SB] = head_norm(k, kg_ref[...]).astype(BF16)
    v = jnp.dot(h, w_ref[:, 2 * D_SB:D_QKV], preferred_element_type=F32)
    qkv_ref[:, 2 * D_SB:D_QKV] = v.astype(BF16)
    rest_ref[...] = jnp.dot(h, w_ref[:, D_QKV:], preferred_element_type=F32)


def _inproj(x, g, w, qg, kg, gsum):
    n = x.shape[0]
    tm = DENSE_TILE
    const = lambda i: (0, 0)
    return pl.pallas_call(
        _inproj_kernel,
        grid=(n // tm,),
        in_specs=[
            pl.BlockSpec((tm, D_MODEL), lambda i: (i, 0)),
            pl.BlockSpec((1, D_MODEL), const),
            pl.BlockSpec((D_MODEL, D_QKV + D_REST), const),
            pl.BlockSpec((1, D_SB), const),
            pl.BlockSpec((1, D_SB), const),
            pl.BlockSpec((D_SB, D_SB), const),
        ],
        out_specs=[
            pl.BlockSpec((tm, D_QKV), lambda i: (i, 0)),
            pl.BlockSpec((tm, D_REST), lambda i: (i, 0)),
        ],
        out_shape=[
            jax.ShapeDtypeStruct((n, D_QKV), BF16),
            jax.ShapeDtypeStruct((n, D_REST), F32),
        ],
        compiler_params=_cparams("arbitrary"),
        name="inproj",
    )(x, g, w, qg, kg, gsum)


def _attn_kernel(q_ref, k_ref, v_ref, o_ref, *scratch):
    def tile(tt, carry):
        rows = pl.ds(pl.multiple_of(tt * ATTN_BQ, ATTN_BQ), ATTN_BQ)
        _attn_tile(pl.program_id(2) * ATTN_TILES + tt, q_ref.at[0, rows], k_ref, v_ref, o_ref.at[0, rows], *scratch)
        return carry
    lax.fori_loop(0, ATTN_TILES, tile, 0)


def _attn_tile(i, q_ref, k_ref, v_ref, o_ref, z0, z1, lbc0, lbc1, hl0, hl1, c0, c1, cm0, cm1, w0, w1, acc_ref):
    bq, bk = ATTN_BQ, ATTN_BK
    heads = (0, 1)
    z_refs, lbc_refs, hilo_refs, c_refs, cmax_refs = (z0, z1), (lbc0, lbc1), (hl0, hl1), (c0, c1), (cm0, cm1)
    w_refs = (w0, w1)
    q = q_ref[...] * jnp.asarray(HEAD_DIM ** -0.5, BF16)
    first = lax.broadcasted_iota(jnp.int32, (bk, LANES), 1) < HEAD_DIM
    t_pos = lax.broadcasted_iota(jnp.int32, (bq, bk), 0) + i * bq
    s_io = lax.broadcasted_iota(jnp.int32, (bq, bk), 1)
    earlier = (lax.broadcasted_iota(jnp.int32, (bk, bk), 0)
               > lax.broadcasted_iota(jnp.int32, (bk, bk), 1))
    neg_later = jnp.where(earlier, -1.0, 0.0).astype(BF16)
    neg_later2 = jnp.concatenate([neg_later, neg_later], axis=0)

    def rows(m):
        return pl.ds(pl.multiple_of(jnp.maximum(m, 0) * bk, bk), bk)

    def causal(m):
        return (s_io + m * bk) < t_pos

    def per_head(x):
        zero = jnp.zeros_like(x)
        return jnp.concatenate([jnp.where(first, x, zero), jnp.where(first, zero, x)], axis=0)

    def stage0(m, s):
        z = lax.dot_general(q, per_head(k_ref[0, rows(m), :]), (((1,), (1,)), ((), ())),
                            preferred_element_type=F32)
        for h in heads:
            z_refs[s][h] = z[:, h * bk:(h + 1) * bk]

    def stage1(m, s, masked):
        mask = causal(m) if masked else None
        for h in heads:
            z = z_refs[s][h]
            c = c_refs[s][h]
            drop = jnp.maximum(z, 0.0) + jnp.log(1.0 + jnp.exp(-jnp.abs(z)))
            lbc_refs[s][h] = (z - drop) + c
            if masked:
                drop = jnp.where(mask, drop, 0.0)
            hi = drop.astype(BF16)
            lo = (drop - hi.astype(F32)).astype(BF16)
            hilo_refs[s][h] = jnp.concatenate([hi, lo], axis=1)
            c_next = c - jnp.sum(drop, axis=-1, keepdims=True)
            c_refs[1 - s][h] = c_next
            cmax_refs[1 - s][h] = jnp.max(c_next.reshape(bq // 8, 8, LANES), axis=0)

    def stage2_sums(s):
        return [jnp.dot(hilo_refs[s][h], neg_later2, preferred_element_type=F32) for h in heads]

    def stage2_apply(m, s, within, masked):
        mask = causal(m) if masked else None
        ws = []
        for h in heads:
            w = jnp.exp(lbc_refs[s][h] + within[h])
            if masked:
                w = jnp.where(mask, w, 0.0)
            ws.append(w.astype(BF16))
        w_refs[s][...] = jnp.concatenate(ws, axis=1)

    def stage3(m, s):
        acc_ref[...] += jnp.dot(w_refs[s][...], per_head(v_ref[0, rows(m), :]), preferred_element_type=F32)

    def alive(m, s):
        return jnp.logical_and(m >= 0, jnp.max(cmax_refs[s][...]) > ATTN_DEAD_LOG)

    top = (i + 1) * (bq // bk) - 1
    acc_ref[...] = jnp.zeros_like(acc_ref)
    c_refs[1][...] = jnp.zeros((2, bq, LANES), F32)

    def step(m, s, masked_next=False, masked=False, has_prev=True):
        live_next = alive(m - 1, 1 - s)
        if has_prev:
            stage3(m + 1, 1 - s)
        within = stage2_sums(s)
        stage0(m - 2, s)
        stage1(m - 1, 1 - s, masked_next)
        stage2_apply(m, s, within, masked)
        return live_next

    stage0(top, 1)
    stage0(top - 1, 0)
    stage1(top, 1, True)
    step(top, 1, masked_next=True, masked=True, has_prev=False)
    step(top - 1, 0, masked=True)

    def body(st):
        m, _ = st
        live_next = lax.cond((m & 1) == 0, functools.partial(step, m, 0), functools.partial(step, m, 1))
        return m - 1, live_next

    m_end, _ = lax.while_loop(lambda st: st[1], body, (top - 2, alive(top - 2, 1)))
    for s in (0, 1):
        pl.when(((m_end + 1) & 1) == s)(functools.partial(stage3, m_end + 1, s))
    o_ref[...] = acc_ref[...]


def _attention(qkv, bsz, seq):
    bq, bk = ATTN_BQ, ATTN_BK
    assert bq == 2 * bk
    pairs = D_SB // LANES
    qkv3 = qkv.reshape(bsz, seq, D_QKV)
    step_rows = ATTN_TILES * bq
    return pl.pallas_call(
        _attn_kernel,
        grid=(bsz, pairs, seq // step_rows),
        in_specs=[
            pl.BlockSpec((1, step_rows, LANES), lambda b, p, i: (b, i, p)),
            pl.BlockSpec((1, seq, LANES), lambda b, p, i: (b, 0, pairs + p)),
            pl.BlockSpec((1, seq, LANES), lambda b, p, i: (b, 0, 2 * pairs + p)),
        ],
        out_specs=pl.BlockSpec((1, step_rows, LANES), lambda b, p, i: (b, i, p)),
        out_shape=jax.ShapeDtypeStruct((bsz, seq, D_SB), F32),
        scratch_shapes=(
            [pltpu.VMEM((2, bq, bk), F32)] * 2
            + [pltpu.VMEM((2, bq, bk), F32)] * 2
            + [pltpu.VMEM((2, bq, 2 * bk), BF16)] * 2
            + [pltpu.VMEM((2, bq, LANES), F32)] * 2
            + [pltpu.VMEM((2, 8, LANES), F32)] * 2
            + [pltpu.VMEM((bq, 2 * bk), BF16)] * 2
            + [pltpu.VMEM((bq, LANES), F32)]
        ),
        compiler_params=_cparams("arbitrary", "arbitrary", "arbitrary"),
        name="attn",
    )(qkv3, qkv3, qkv3)


def _shift_rows(x, d, fill):
    t = x.shape[0]
    if d % 8 == 0:
        return jnp.concatenate([jnp.full((d, x.shape[1]), fill, x.dtype), x[:t - d]], axis=0)
    rows = lax.broadcasted_iota(jnp.int32, x.shape, 0)
    return jnp.where(rows < d, fill, pltpu.roll(x, d, axis=0))


def _mixers_kernel(rest_ref, lcw_ref, lcb_ref, wg_ref, ba_ref, bx_ref, lam_ref,
                   cw_ref, cb_ref, lng_ref, lnb_ref, pw_ref, pb_ref,
                   y_ref, ltail_ref, h_ref, ctail_ref, lwin_ref, cwin_ref):
    t = MIX_TILE

    @pl.when(pl.program_id(1) == 0)
    def _():
        ltail_ref[...] = jnp.zeros_like(ltail_ref)
        h_ref[...] = jnp.zeros_like(h_ref)
        ctail_ref[...] = jnp.zeros_like(ctail_ref)

    xr = rest_ref[0, :, 0:D_LRU]
    xg = rest_ref[0, :, D_LRU:2 * D_LRU]
    val = rest_ref[0, :, 2 * D_LRU:2 * D_LRU + D_CONF]
    gate = rest_ref[0, :, 2 * D_LRU + D_CONF:]

    lwin_ref[0:LRU_HALO, :] = ltail_ref[...]
    lwin_ref[LRU_HALO:, :] = xr
    ltail_ref[...] = xr[t - LRU_HALO:, :]
    conv = lcb_ref[...]
    for k in range(LRU_CONV_WIDTH):
        off = LRU_HALO - (LRU_CONV_WIDTH - 1) + k
        conv = conv + lcw_ref[k:k + 1, :] * lwin_ref[off:off + t, :]
    gates = jnp.dot(conv.astype(BF16), wg_ref[...], preferred_element_type=F32)
    gate_r = gates[:, 0:D_LRU] + ba_ref[...]
    gate_i = gates[:, D_LRU:] + bx_ref[...]
    log_a = LRU_C * jax.nn.sigmoid(gate_r) * jax.nn.log_sigmoid(lam_ref[...])
    a = jnp.exp(log_a)
    th = jnp.tanh(log_a)
    u = jnp.sqrt(-2.0 * th / (1.0 - th)) * (jax.nn.sigmoid(gate_i) * conv)
    d = 1
    while d < t:
        u = a * _shift_rows(u, d, 0.0) + u
        a = a * _shift_rows(a, d, 1.0)
        d *= 2
    h = a * h_ref[...] + u
    h_ref[...] = h[t - 1:t, :]
    y_ref[0, :, 0:D_LRU] = h * jax.nn.gelu(xg)

    glu = val * jax.nn.sigmoid(gate)
    cwin_ref[0:CONF_HALO, :] = ctail_ref[...]
    cwin_ref[CONF_HALO:, :] = glu
    ctail_ref[...] = glu[t - CONF_HALO:, :]
    acc = cb_ref[...]
    first_off = CONF_HALO - (CONF_CONV_WIDTH - 1)
    for r in range(8):
        offs = [o for o in range(first_off, first_off + CONF_CONV_WIDTH) if o % 8 == r]
        span = t + (8 if r else 0)
        part = None
        for o in offs:
            term = cw_ref[o - first_off:o - first_off + 1, :] * cwin_ref[o - r:o - r + span, :]
            part = term if part is None else part + term
        acc = acc + part[r:r + t, :]
    mu = jnp.mean(acc, axis=-1, keepdims=True)
    cen = acc - mu
    var = jnp.mean(cen * cen, axis=-1, keepdims=True)
    ln = cen * lax.rsqrt(var + EPS) * lng_ref[...] + lnb_ref[...]
    sw = (ln * jax.nn.sigmoid(ln)).astype(BF16)
    y_ref[0, :, D_LRU:] = jnp.dot(sw, pw_ref[...], preferred_element_type=F32) + pb_ref[...]


def _mixers(rest, bsz, seq, lcw, lcb, wgates, ba, bx, lam, cw, cb, lng, lnb, pw, pb):
    t = MIX_TILE
    rest3 = rest.reshape(bsz, seq, D_REST)
    const = lambda b, i: (0, 0)
    row = lambda d: pl.BlockSpec((1, d), const)
    return pl.pallas_call(
        _mixers_kernel,
        grid=(bsz, seq // t),
        in_specs=[
            pl.BlockSpec((1, t, D_REST), lambda b, i: (b, i, 0)),
            pl.BlockSpec((LRU_CONV_WIDTH, D_LRU), const), row(D_LRU),
            pl.BlockSpec((D_LRU, 2 * D_LRU), const), row(D_LRU), row(D_LRU), row(D_LRU),
            pl.BlockSpec((CONF_CONV_WIDTH, D_CONF), const), row(D_CONF),
            row(D_CONF), row(D_CONF),
            pl.BlockSpec((D_CONF, D_CONF), const), row(D_CONF),
        ],
        out_specs=pl.BlockSpec((1, t, D_LRU + D_CONF), lambda b, i: (b, i, 0)),
        out_shape=jax.ShapeDtypeStruct((bsz, seq, D_LRU + D_CONF), F32),
        scratch_shapes=[
            pltpu.VMEM((LRU_HALO, D_LRU), F32),
            pltpu.VMEM((1, D_LRU), F32),
            pltpu.VMEM((CONF_HALO, D_CONF), F32),
            pltpu.VMEM((LRU_HALO + t, D_LRU), F32),
            pltpu.VMEM((CONF_HALO + t, D_CONF), F32),
        ],
        compiler_params=_cparams("arbitrary", "arbitrary"),
        name="mixers",
    )(rest3, lcw, lcb, wgates, ba, bx, lam, cw, cb, lng, lnb, pw, pb)


ROUTE_GATE_A, ROUTE_GATE_B, ROUTE_BUCKET, ROUTE_RANK = range(4)
ROUTE_ROWS = 8
PAIRS_PER_GROUP = EXPERTS_PER_GROUP * (EXPERTS_PER_GROUP - 1) // 2
N_BUCKETS = N_GROUPS * PAIRS_PER_GROUP
assert N_BUCKETS <= LANES
D_ROUTED = D_PACKED + LANES
ROUTER_FINE_COL0 = N_GROUPS


def _outproj_kernel(ya_ref, ybc_ref, x_ref, gout_ref, wout_ref, g2_ref, wr_ref, ltri_ref,
                    x1_ref, h2_ref, route_t_ref, counts_ref, cnt_ref):
    @pl.when(pl.program_id(0) == 0)
    def _():
        cnt_ref[...] = jnp.zeros_like(cnt_ref)

    y = jnp.concatenate(
        [_rms(ya_ref[...]), _rms(ybc_ref[:, 0:D_LRU]), _rms(ybc_ref[:, D_LRU:])], axis=1)
    y = (y * gout_ref[...]).astype(BF16)
    x1 = x_ref[...] + jnp.dot(y, wout_ref[...], preferred_element_type=F32)
    x1_ref[...] = x1
    h2 = (_rms(x1) * g2_ref[...]).astype(BF16)
    h2_ref[:, 0:D_PACKED] = _pack_bf16_pairs(h2)
    lf = jnp.dot(h2, wr_ref[...], preferred_element_type=F32)

    tm = lf.shape[0]
    col_i = lax.broadcasted_iota(jnp.int32, (tm, LANES), 1)
    col = col_i.astype(F32)
    neg = -jnp.inf
    big = float(LANES)
    is_c = col_i < N_GROUPS
    lc = jnp.where(is_c, lf, neg)
    mc = jnp.max(lc, axis=-1, keepdims=True)
    grp = jnp.min(jnp.where(lc == mc, col, big), axis=-1, keepdims=True)
    w_grp = 1.0 / jnp.sum(jnp.where(is_c, jnp.exp(lf - mc), 0.0), axis=-1, keepdims=True)
    lo = ROUTER_FINE_COL0 + EXPERTS_PER_GROUP * grp
    in_grp = jnp.logical_and(col >= lo, col < lo + EXPERTS_PER_GROUP)
    l1 = jnp.where(in_grp, lf, neg)
    v1 = jnp.max(l1, axis=-1, keepdims=True)
    i1 = jnp.min(jnp.where(l1 == v1, col, big), axis=-1, keepdims=True)
    sel1 = col == i1
    l2 = jnp.where(sel1, neg, l1)
    v2 = jnp.max(l2, axis=-1, keepdims=True)
    i2 = jnp.min(jnp.where(l2 == v2, col, big), axis=-1, keepdims=True)
    sel2 = col == i2
    e21 = jnp.exp(v2 - v1)
    g1 = w_grp / (1.0 + e21)
    g2 = w_grp * e21 / (1.0 + e21)

    swap = i2 < i1
    a = jnp.where(swap, i2, i1) - lo
    b = jnp.where(swap, i1, i2) - lo
    gate_a = jnp.where(swap, g2, g1)
    gate_b = jnp.where(swap, g1, g2)
    pair = a * (2 * EXPERTS_PER_GROUP - 1 - a) * 0.5 + (b - a - 1.0)
    bucket = grp * PAIRS_PER_GROUP + pair

    sel = col == bucket
    onehot = jnp.where(sel, 1.0, 0.0)
    before = jnp.dot(ltri_ref[...], onehot.astype(BF16), preferred_element_type=F32)
    cnt = cnt_ref[...]
    rank = jnp.sum(jnp.where(sel, before + cnt, 0.0), axis=-1, keepdims=True)
    cnt_ref[...] = cnt + jnp.sum(onehot, axis=0, keepdims=True)

    out = jnp.zeros((tm, LANES), F32)
    for lane_id, val in ((ROUTE_GATE_A, gate_a), (ROUTE_GATE_B, gate_b), (ROUTE_BUCKET, bucket), (ROUTE_RANK, rank)):
        out = jnp.where(col_i == lane_id, val, out)
    h2_ref[:, D_PACKED:] = pltpu.bitcast(out, jnp.uint32)
    route_t_ref[...] = out.T[0:ROUTE_ROWS, :]
    counts_ref[...] = cnt_ref[...]


def _outproj(ya, ybc, x, gout, wout, g2, wr, ltri):
    n = x.shape[0]
    tm = DENSE_TILE
    const = lambda i: (0, 0)
    tile = lambda d: pl.BlockSpec((tm, d), lambda i: (i, 0))
    return pl.pallas_call(
        _outproj_kernel,
        grid=(n // tm,),
        in_specs=[
            tile(D_SB), tile(D_LRU + D_CONF), tile(D_MODEL),
            pl.BlockSpec((1, D_MODEL), const),
            pl.BlockSpec((D_MODEL, D_MODEL), const),
            pl.BlockSpec((1, D_MODEL), const),
            pl.BlockSpec((D_MODEL, LANES), const),
            pl.BlockSpec((tm, tm), const),
        ],
        out_specs=[tile(D_MODEL), tile(D_ROUTED),
                   pl.BlockSpec((ROUTE_ROWS, tm), lambda i: (0, i)),
                   pl.BlockSpec((1, LANES), const)],
        out_shape=[
            jax.ShapeDtypeStruct((n, D_MODEL), F32),
            jax.ShapeDtypeStruct((n, D_ROUTED), jnp.uint32),
            jax.ShapeDtypeStruct((ROUTE_ROWS, n), F32),
            jax.ShapeDtypeStruct((1, LANES), F32),
        ],
        scratch_shapes=[pltpu.VMEM((1, LANES), F32)],
        compiler_params=_cparams("arbitrary"),
        name="outproj",
    )(ya, ybc, x, gout, wout, g2, wr, ltri)


def _row_copy(src_ref, src_row, dst_ref, dst_row, sem):
    return pltpu.make_async_copy(src_ref.at[pl.ds(src_row, 1)], dst_ref.at[pl.ds(dst_row, 1)], sem)


ZERO_CHUNK_ROWS = (128, 64, 32, 16, 8)


def _dispatch_kernel(pos_ref, padlo_ref, padhi_ref, nv_ref, h2_ref, xs_ref, zero_ref, sem, zsem):
    tm = ROW_TILE
    i = pl.program_id(0)
    base = i * tm

    @pl.when(i == 0)
    def _():
        zero_ref[...] = jnp.zeros_like(zero_ref)

        def fill_tail(action, e, carry):
            def go(copy):
                copy.start() if action == "start" else copy.wait()

            lo, hi = padlo_ref[e], padhi_ref[e]
            lo8 = jnp.minimum((lo + 7) // 8 * 8, hi)

            def one_row(r, c):
                go(_row_copy(zero_ref, 0, xs_ref, r, zsem))
                return c
            lax.fori_loop(lo, lo8, one_row, 0)
            groups = (hi - lo8) // 8
            cur = lo8
            for rows in ZERO_CHUNK_ROWS:
                take = (groups & (rows // 8)) != 0

                @pl.when(take)
                def _(rows=rows, cur=cur):
                    dst = xs_ref.at[pl.ds(pl.multiple_of(cur, 8), rows)]
                    go(pltpu.make_async_copy(zero_ref.at[pl.ds(0, rows)], dst, zsem))
                cur = cur + jnp.where(take, rows, 0)
            return carry

        def idle_block(action, b, carry):
            rows = pl.ds(pl.multiple_of(b * MOE_BLOCK, MOE_BLOCK), MOE_BLOCK)
            copy = pltpu.make_async_copy(zero_ref, xs_ref.at[rows], zsem)
            copy.start() if action == "start" else copy.wait()
            return carry

        n_blocks = xs_ref.shape[0] // MOE_BLOCK
        for action in ("start", "wait"):
            lax.fori_loop(0, N_BUCKETS, functools.partial(fill_tail, action), 0)
            lax.fori_loop(nv_ref[0], n_blocks, functools.partial(idle_block, action), 0)

    def start(r, c):
        _row_copy(h2_ref, r, xs_ref, pos_ref[base + r], sem).start()
        return c
    lax.fori_loop(0, tm, start, 0, unroll=DMA_UNROLL)

    def wait(r, c):
        _row_copy(h2_ref, r, xs_ref, pos_ref[base + r], sem).wait()
        return c
    lax.fori_loop(0, tm, wait, 0, unroll=DMA_UNROLL)


def _dispatch(pos, padlo, padhi, n_valid, h2, n_slots):
    n = h2.shape[0]
    tm = ROW_TILE
    assert sum(ZERO_CHUNK_ROWS) + 8 == MOE_BLOCK
    return pl.pallas_call(
        _dispatch_kernel,
        grid_spec=pltpu.PrefetchScalarGridSpec(
            num_scalar_prefetch=4,
            grid=(n // tm,),
            in_specs=[pl.BlockSpec((tm, D_ROUTED), lambda i, *_: (i, 0))],
            out_specs=pl.BlockSpec(memory_space=pl.ANY),
            scratch_shapes=[
                pltpu.VMEM((MOE_BLOCK, D_ROUTED), jnp.uint32),
                pltpu.SemaphoreType.DMA(()),
                pltpu.SemaphoreType.DMA(()),
            ],
        ),
        out_shape=jax.ShapeDtypeStruct((n_slots, D_ROUTED), jnp.uint32),
        compiler_params=_cparams("arbitrary"),
        name="dispatch",
    )(pos, padlo, padhi, n_valid, h2)


def _experts_kernel(ea_ref, eb_ref, nv_ref, xs_ref, wga_ref, wua_ref, wda_ref, wgb_ref, wub_ref, wdb_ref,
                    ys_ref, *bf_refs):
    i = pl.program_id(0)
    live = i < nv_ref[0]
    prev = jnp.maximum(i - 1, 0)
    sides = ((ea_ref, (wga_ref, wua_ref, wda_ref), bf_refs[0:3]),
             (eb_ref, (wgb_ref, wub_ref, wdb_ref), bf_refs[3:6]))

    for e_ref, w_refs, w_bf in sides:
        changed = jnp.logical_or(i == 0, e_ref[i] != e_ref[prev])

        @pl.when(jnp.logical_and(live, changed))
        def _(w_refs=w_refs, w_bf=w_bf):
            for src, dst in zip(w_refs, w_bf):
                dst[...] = src[0, 0].astype(BF16)

    @pl.when(live)
    def _():
        x = _unpack_bf16_pairs(xs_ref[:, 0:D_PACKED])
        record = pltpu.bitcast(xs_ref[:, D_PACKED:], F32)
        lane = lax.broadcasted_iota(jnp.int32, record.shape, 1)
        y = None
        for gate_lane, (_, _, (wg_bf, wu_bf, wd_bf)) in zip((ROUTE_GATE_A, ROUTE_GATE_B), sides):
            gate = jnp.sum(jnp.where(lane == gate_lane, record, 0.0), axis=-1, keepdims=True)
            g = jnp.dot(x, wg_bf[...], preferred_element_type=F32)
            u = jnp.dot(x, wu_bf[...], preferred_element_type=F32)
            act = (g * jax.nn.sigmoid(g) * u).astype(BF16)
            part = jnp.dot(act, wd_bf[...], preferred_element_type=F32) * gate
            y = part if y is None else y + part
        ys_ref[...] = y

    @pl.when(jnp.logical_not(live))
    def _():
        ys_ref[...] = jnp.zeros_like(ys_ref)


def _experts(block_ea, block_eb, n_valid, xs, wg, wu, wd, layer):
    n_slots = xs.shape[0]
    n_blocks = n_slots // MOE_BLOCK
    side_a = lambda i, ea, eb, nv: (layer, ea[i], 0, 0)
    side_b = lambda i, ea, eb, nv: (layer, eb[i], 0, 0)
    up = lambda index: pl.BlockSpec((1, 1, D_MODEL, D_EXPERT), index)
    down = lambda index: pl.BlockSpec((1, 1, D_EXPERT, D_MODEL), index)
    return pl.pallas_call(
        _experts_kernel,
        grid_spec=pltpu.PrefetchScalarGridSpec(
            num_scalar_prefetch=3,
            grid=(n_blocks,),
            in_specs=[
                pl.BlockSpec((MOE_BLOCK, D_ROUTED), lambda i, ea, eb, nv: (jnp.minimum(i, nv[0] - 1), 0)),
                up(side_a), up(side_a), down(side_a),
                up(side_b), up(side_b), down(side_b),
            ],
            out_specs=pl.BlockSpec((MOE_BLOCK, D_MODEL), lambda i, ea, eb, nv: (i, 0)),
            scratch_shapes=[
                pltpu.VMEM((D_MODEL, D_EXPERT), BF16),
                pltpu.VMEM((D_MODEL, D_EXPERT), BF16),
                pltpu.VMEM((D_EXPERT, D_MODEL), BF16),
            ] * 2,
        ),
        out_shape=jax.ShapeDtypeStruct((n_slots, D_MODEL), F32),
        compiler_params=_cparams("arbitrary"),
        name="experts",
    )(block_ea, block_eb, n_valid, xs, wg, wu, wd, wg, wu, wd)


def _combine_kernel(pos_ref, ys_ref, x1_ref, out_ref, buf_ref, sems):
    tm = ROW_TILE
    i = pl.program_id(0)
    n_steps = pl.num_programs(0)

    def gather(step, slot, action):
        base = step * tm

        def body(r, c):
            cp = pltpu.make_async_copy(ys_ref.at[pl.ds(pos_ref[base + r], 1)],
                                       buf_ref.at[slot, pl.ds(r, 1)], sems.at[slot])
            cp.start() if action == "start" else cp.wait()
            return c
        lax.fori_loop(0, tm, body, 0, unroll=DMA_UNROLL)

    slot = lax.rem(i, 2)

    @pl.when(i == 0)
    def _():
        gather(0, 0, "start")

    @pl.when(i + 1 < n_steps)
    def _():
        gather(i + 1, 1 - slot, "start")

    gather(i, slot, "wait")
    out_ref[...] = x1_ref[...] + buf_ref[slot]


def _combine(pos, ys, x1):
    n = x1.shape[0]
    tm = ROW_TILE
    return pl.pallas_call(
        _combine_kernel,
        grid_spec=pltpu.PrefetchScalarGridSpec(
            num_scalar_prefetch=1,
            grid=(n // tm,),
            in_specs=[
                pl.BlockSpec(memory_space=pl.ANY),
                pl.BlockSpec((tm, D_MODEL), lambda i, *_: (i, 0)),
            ],
            out_specs=pl.BlockSpec((tm, D_MODEL), lambda i, *_: (i, 0)),
            scratch_shapes=[
                pltpu.VMEM((2, tm, D_MODEL), F32),
                pltpu.SemaphoreType.DMA((2,)),
            ],
        ),
        out_shape=jax.ShapeDtypeStruct((n, D_MODEL), F32),
        compiler_params=_cparams("arbitrary"),
        name="combine",
    )(pos, ys, x1)


def _bucket_experts():
    ea, eb = [], []
    for g in range(N_GROUPS):
        for a in range(EXPERTS_PER_GROUP):
            for b in range(a + 1, EXPERTS_PER_GROUP):
                ea.append(g * EXPERTS_PER_GROUP + a)
                eb.append(g * EXPERTS_PER_GROUP + b)
    return jnp.asarray(ea, jnp.int32), jnp.asarray(eb, jnp.int32)


def _slot_plan(route_t, counts_row, n_tok):
    bucket = route_t[ROUTE_BUCKET].astype(jnp.int32)
    rank = route_t[ROUTE_RANK].astype(jnp.int32)
    buckets = jnp.arange(N_BUCKETS, dtype=jnp.int32)
    counts = counts_row[0, 0:N_BUCKETS].astype(jnp.int32)
    padded = (counts + MOE_BLOCK - 1) // MOE_BLOCK * MOE_BLOCK
    padded_ends = jnp.cumsum(padded)
    padded_starts = padded_ends - padded
    pos = jnp.sum(jnp.where(bucket[None, :] == buckets[:, None], padded_starts[:, None], 0), axis=0) + rank
    n_blocks = -(-n_tok // MOE_BLOCK) + N_BUCKETS
    n_valid = (padded_ends[-1] // MOE_BLOCK).astype(jnp.int32)
    block_start = jnp.arange(n_blocks, dtype=jnp.int32) * MOE_BLOCK
    block_start = jnp.minimum(block_start, padded_ends[-1] - MOE_BLOCK)
    block_bucket = jnp.sum((padded_ends[None, :] <= block_start[:, None]).astype(jnp.int32), axis=1)
    block_bucket = jnp.minimum(block_bucket, N_BUCKETS - 1)
    in_bucket = block_bucket[:, None] == buckets[None, :]
    ea, eb = _bucket_experts()
    return dict(pos=pos, padlo=(padded_starts + counts).astype(jnp.int32), padhi=padded_ends.astype(jnp.int32),
                block_ea=jnp.sum(jnp.where(in_bucket, ea[None, :], 0), axis=1),
                block_eb=jnp.sum(jnp.where(in_bucket, eb[None, :], 0), axis=1),
                n_valid=n_valid.reshape(1), n_slots=n_blocks * MOE_BLOCK)


def _block_diag(w):
    nb, r, c = w.shape
    eye = jnp.eye(nb, dtype=w.dtype)
    return (w[:, :, None, :] * eye[:, None, :, None]).reshape(nb * r, nb * c)


def kernel(x, norm1_g, w_in, q_norm_g, k_norm_g, lru_conv_w, lru_conv_b, lru_wa, lru_ba, lru_wx, lru_bx, lru_lambda, conf_dw_w, conf_dw_b, conf_ln_g, conf_ln_b, conf_pw_w, conf_pw_b, out_norm_g, w_out, norm2_g, router_coarse, router_fine, exp_w_gate, exp_w_up, exp_w_down):
    bsz, seq, d = x.shape
    n = bsz * seq
    depth = w_in.shape[0]
    assert d == D_MODEL and n % DENSE_TILE == 0 and n % ROW_TILE == 0
    assert seq % MIX_TILE == 0 and seq % (ATTN_TILES * ATTN_BQ) == 0

    heads = D_SB // HEAD_DIM
    head_of = jnp.arange(D_SB, dtype=jnp.int32) // HEAD_DIM
    gsum = (head_of[:, None] == head_of[None, :]).astype(BF16)
    t_io = jnp.arange(DENSE_TILE, dtype=jnp.int32)
    ltri = (t_io[None, :] < t_io[:, None]).astype(BF16)
    row = lambda v: v.reshape(1, -1)

    xf = x.reshape(n, d)
    for l in range(depth):
        qkv, rest = _inproj(xf, row(norm1_g[l]), w_in[l].astype(BF16),
                            row(jnp.tile(q_norm_g[l], heads)), row(jnp.tile(k_norm_g[l], heads)), gsum)
        ya = _attention(qkv, bsz, seq).reshape(n, D_SB)
        wgates = jnp.concatenate([_block_diag(lru_wa[l]), _block_diag(lru_wx[l])], axis=1).astype(BF16)
        ybc = _mixers(rest, bsz, seq, lru_conv_w[l], row(lru_conv_b[l]), wgates,
                      row(lru_ba[l]), row(lru_bx[l]), row(lru_lambda[l]),
                      conf_dw_w[l], row(conf_dw_b[l]), row(conf_ln_g[l]), row(conf_ln_b[l]),
                      conf_pw_w[l].astype(BF16), row(conf_pw_b[l])).reshape(n, D_LRU + D_CONF)
        wr = jnp.concatenate([router_coarse[l], router_fine[l]], axis=1)
        wr = jnp.pad(wr, ((0, 0), (0, LANES - wr.shape[1]))).astype(BF16)
        x1, h2, route_t, counts = _outproj(ya, ybc, xf, row(out_norm_g[l]), w_out[l].astype(BF16),
                                           row(norm2_g[l]), wr, ltri)
        plan = _slot_plan(route_t, counts, n)
        xs = _dispatch(plan["pos"], plan["padlo"], plan["padhi"], plan["n_valid"], h2, plan["n_slots"])
        ys = _experts(plan["block_ea"], plan["block_eb"], plan["n_valid"], xs,
                      exp_w_gate, exp_w_up, exp_w_down, l)
        xf = _combine(plan["pos"], ys, x1)
    return xf.reshape(bsz, seq, d)
```

```python
import functools

import jax
import jax.numpy as jnp
from jax import lax
from jax.experimental import pallas as pl
from jax.experimental.pallas import tpu as pltpu

F32 = jnp.float32
BF16 = jnp.bfloat16

D_MODEL = 1024
HEAD_DIM = 64
D_SB = 512
D_LRU = 256
LRU_BLOCKS = 4
LRU_CONV_WIDTH = 4
LRU_C = 8.0
D_CONF = 256
CONF_CONV_WIDTH = 31
D_QKV = 3 * D_SB
D_REST = 2 * D_LRU + 2 * D_CONF
N_GROUPS = 4
EXPERTS_PER_GROUP = 8
N_EXPERTS = N_GROUPS * EXPERTS_PER_GROUP
D_EXPERT = 512
D_PACKED = D_MODEL // 2
MOE_BLOCK = 256
EPS = 1e-6

LANES = 128
ROW_TILE = 512
DENSE_TILE = 512
ATTN_BQ = 256
ATTN_BK = 128
ATTN_TILES = 4
MIX_TILE = 512
CONF_HALO = 32
DMA_UNROLL = 8
LRU_HALO = 8
ATTN_DEAD_LOG = -110.0
VMEM_LIMIT = 48 * 1024 * 1024
EXPERTS_VMEM_LIMIT = 56 * 1024 * 1024


def _cparams(*sem):
    return pltpu.CompilerParams(dimension_semantics=sem, vmem_limit_bytes=VMEM_LIMIT)


def _rms(y):
    return y * lax.rsqrt(jnp.mean(y * y, axis=-1, keepdims=True) + EPS)


def _pack_bf16_pairs(x):
    c = x.shape[1] // 2
    bits = pltpu.bitcast(x.astype(F32), jnp.uint32)
    return (bits[:, :c] >> 16) | (bits[:, c:] & jnp.uint32(0xFFFF0000))


def _unpack_bf16_pairs(words):
    low = pltpu.bitcast(words << 16, F32)
    high = pltpu.bitcast(words & jnp.uint32(0xFFFF0000), F32)
    return jnp.concatenate([low, high], axis=1).astype(BF16)


def _inproj_kernel(x_ref, g_ref, w_ref, qg_ref, kg_ref, gsum_ref, qkv_ref, rest_ref):
    h = (_rms(x_ref[...]) * g_ref[...]).astype(BF16)

    def head_norm(p, gain):
        s = jnp.dot((p * p).astype(BF16), gsum_ref[...], preferred_element_type=F32)
        return p * lax.rsqrt(s * (1.0 / HEAD_DIM) + EPS) * gain

    q = jnp.dot(h, w_ref[:, 0:D_SB], preferred_element_type=F32)
    qkv_ref[:, 0:D_SB] = head_norm(q, qg_ref[...]).astype(BF16)
    k = jnp.dot(h, w_ref[:, D_SB:2 * D_SB], preferred_element_type=F32)
    qkv_ref[:, D_SB:2 * D_SB] = head_norm(k, kg_ref[...]).astype(BF16)
    v = jnp.dot(h, w_ref[:, 2 * D_SB:D_QKV], preferred_element_type=F32)
    qkv_ref[:, 2 * D_SB:D_QKV] = v.astype(BF16)
    rest_ref[...] = jnp.dot(h, w_ref[:, D_QKV:], preferred_element_type=F32)


def _inproj(x, g, w, qg, kg, gsum):
    n = x.shape[0]
    tm = DENSE_TILE
    const = lambda i: (0, 0)
    return pl.pallas_call(
        _inproj_kernel,
        grid=(n // tm,),
        in_specs=[
            pl.BlockSpec((tm, D_MODEL), lambda i: (i, 0)),
            pl.BlockSpec((1, D_MODEL), const),
            pl.BlockSpec((D_MODEL, D_QKV + D_REST), const),
            pl.BlockSpec((1, D_SB), const),
            pl.BlockSpec((1, D_SB), const),
            pl.BlockSpec((D_SB, D_SB), const),
        ],
        out_specs=[
            pl.BlockSpec((tm, D_QKV), lambda i: (i, 0)),
            pl.BlockSpec((tm, D_REST), lambda i: (i, 0)),
        ],
        out_shape=[
            jax.ShapeDtypeStruct((n, D_QKV), BF16),
            jax.ShapeDtypeStruct((n, D_REST), F32),
        ],
        compiler_params=_cparams("arbitrary"),
        name="inproj",
    )(x, g, w, qg, kg, gsum)


def _attn_kernel(q_ref, k_ref, v_ref, o_ref, *scratch):
    def tile(tt, carry):
        rows = pl.ds(pl.multiple_of(tt * ATTN_BQ, ATTN_BQ), ATTN_BQ)
        _attn_tile(pl.program_id(2) * ATTN_TILES + tt, q_ref.at[0, rows], k_ref, v_ref, o_ref.at[0, rows], *scratch)
        return carry
    lax.fori_loop(0, ATTN_TILES, tile, 0)


def _attn_tile(i, q_ref, k_ref, v_ref, o_ref, z0, z1, lbc0, lbc1, hl0, hl1, c0, c1, cm0, cm1, w0, w1, acc_ref):
    bq, bk = ATTN_BQ, ATTN_BK
    heads = (0, 1)
    z_refs, lbc_refs, hilo_refs, c_refs, cmax_refs = (z0, z1), (lbc0, lbc1), (hl0, hl1), (c0, c1), (cm0, cm1)
    w_refs = (w0, w1)
    q = q_ref[...] * jnp.asarray(HEAD_DIM ** -0.5, BF16)
    first = lax.broadcasted_iota(jnp.int32, (bk, LANES), 1) < HEAD_DIM
    t_pos = lax.broadcasted_iota(jnp.int32, (bq, bk), 0) + i * bq
    s_io = lax.broadcasted_iota(jnp.int32, (bq, bk), 1)
    earlier = (lax.broadcasted_iota(jnp.int32, (bk, bk), 0)
               > lax.broadcasted_iota(jnp.int32, (bk, bk), 1))
    neg_later = jnp.where(earlier, -1.0, 0.0).astype(BF16)
    neg_later2 = jnp.concatenate([neg_later, neg_later], axis=0)

    def rows(m):
        return pl.ds(pl.multiple_of(jnp.maximum(m, 0) * bk, bk), bk)

    def causal(m):
        return (s_io + m * bk) < t_pos

    def per_head(x):
        zero = jnp.zeros_like(x)
        return jnp.concatenate([jnp.where(first, x, zero), jnp.where(first, zero, x)], axis=0)

    def stage0(m, s):
        z = lax.dot_general(q, per_head(k_ref[0, rows(m), :]), (((1,), (1,)), ((), ())),
                            preferred_element_type=F32)
        for h in heads:
            z_refs[s][h] = z[:, h * bk:(h + 1) * bk]

    def stage1(m, s, masked):
        mask = causal(m) if masked else None
        for h in heads:
            z = z_refs[s][h]
            c = c_refs[s][h]
            drop = jnp.maximum(z, 0.0) + jnp.log(1.0 + jnp.exp(-jnp.abs(z)))
            lbc_refs[s][h] = (z - drop) + c
            if masked:
                drop = jnp.where(mask, drop, 0.0)
            hi = drop.astype(BF16)
            lo = (drop - hi.astype(F32)).astype(BF16)
            hilo_refs[s][h] = jnp.concatenate([hi, lo], axis=1)
            c_next = c - jnp.sum(drop, axis=-1, keepdims=True)
            c_refs[1 - s][h] = c_next
            cmax_refs[1 - s][h] = jnp.max(c_next.reshape(bq // 8, 8, LANES), axis=0)

    def stage2_sums(s):
        return [jnp.dot(hilo_refs[s][h], neg_later2, preferred_element_type=F32) for h in heads]

    def stage2_apply(m, s, within, masked):
        mask = causal(m) if masked else None
        ws = []
        for h in heads:
            w = jnp.exp(lbc_refs[s][h] + within[h])
            if masked:
                w = jnp.where(mask, w, 0.0)
            ws.append(w.astype(BF16))
        w_refs[s][...] = jnp.concatenate(ws, axis=1)

    def stage3(m, s):
        acc_ref[...] += jnp.dot(w_refs[s][...], per_head(v_ref[0, rows(m), :]), preferred_element_type=F32)

    def alive(m, s):
        return jnp.logical_and(m >= 0, jnp.max(cmax_refs[s][...]) > ATTN_DEAD_LOG)

    top = (i + 1) * (bq // bk) - 1
    acc_ref[...] = jnp.zeros_like(acc_ref)
    c_refs[1][...] = jnp.zeros((2, bq, LANES), F32)

    def step(m, s, masked_next=False, masked=False, has_prev=True):
        live_next = alive(m - 1, 1 - s)
        if has_prev:
            stage3(m + 1, 1 - s)
        within = stage2_sums(s)
        stage0(m - 2, s)
        stage1(m - 1, 1 - s, masked_next)
        stage2_apply(m, s, within, masked)
        return live_next

    stage0(top, 1)
    stage0(top - 1, 0)
    stage1(top, 1, True)
    step(top, 1, masked_next=True, masked=True, has_prev=False)
    step(top - 1, 0, masked=True)

    def body(st):
        m, _ = st
        live_next = lax.cond((m & 1) == 0, functools.partial(step, m, 0), functools.partial(step, m, 1))
        return m - 1, live_next

    m_end, _ = lax.while_loop(lambda st: st[1], body, (top - 2, alive(top - 2, 1)))
    for s in (0, 1):
        pl.when(((m_end + 1) & 1) == s)(functools.partial(stage3, m_end + 1, s))
    o_ref[...] = acc_ref[...]


def _attention(qkv, bsz, seq):
    bq, bk = ATTN_BQ, ATTN_BK
    assert bq == 2 * bk
    pairs = D_SB // LANES
    qkv3 = qkv.reshape(bsz, seq, D_QKV)
    step_rows = ATTN_TILES * bq
    return pl.pallas_call(
        _attn_kernel,
        grid=(bsz, pairs, seq // step_rows),
        in_specs=[
            pl.BlockSpec((1, step_rows, LANES), lambda b, p, i: (b, i, p)),
            pl.BlockSpec((1, seq, LANES), lambda b, p, i: (b, 0, pairs + p)),
            pl.BlockSpec((1, seq, LANES), lambda b, p, i: (b, 0, 2 * pairs + p)),
        ],
        out_specs=pl.BlockSpec((1, step_rows, LANES), lambda b, p, i: (b, i, p)),
        out_shape=jax.ShapeDtypeStruct((bsz, seq, D_SB), F32),
        scratch_shapes=(
            [pltpu.VMEM((2, bq, bk), F32)] * 2
            + [pltpu.VMEM((2, bq, bk), F32)] * 2
            + [pltpu.VMEM((2, bq, 2 * bk), BF16)] * 2
            + [pltpu.VMEM((2, bq, LANES), F32)] * 2
            + [pltpu.VMEM((2, 8, LANES), F32)] * 2
            + [pltpu.VMEM((bq, 2 * bk), BF16)] * 2
            + [pltpu.VMEM((bq, LANES), F32)]
        ),
        compiler_params=_cparams("arbitrary", "arbitrary", "arbitrary"),
        name="attn",
    )(qkv3, qkv3, qkv3)


def _shift_rows(x, d, fill):
    t = x.shape[0]
    if d % 8 == 0:
        return jnp.concatenate([jnp.full((d, x.shape[1]), fill, x.dtype), x[:t - d]], axis=0)
    rows = lax.broadcasted_iota(jnp.int32, x.shape, 0)
    return jnp.where(rows < d, fill, pltpu.roll(x, d, axis=0))


def _mixers_kernel(rest_ref, lcw_ref, lcb_ref, wg_ref, ba_ref, bx_ref, lam_ref,
                   cw_ref, cb_ref, lng_ref, lnb_ref, pw_ref, pb_ref,
                   y_ref, ltail_ref, h_ref, ctail_ref, lwin_ref, cwin_ref):
    t = MIX_TILE

    @pl.when(pl.program_id(1) == 0)
    def _():
        ltail_ref[...] = jnp.zeros_like(ltail_ref)
        h_ref[...] = jnp.zeros_like(h_ref)
        ctail_ref[...] = jnp.zeros_like(ctail_ref)

    xr = rest_ref[0, :, 0:D_LRU]
    xg = rest_ref[0, :, D_LRU:2 * D_LRU]
    val = rest_ref[0, :, 2 * D_LRU:2 * D_LRU + D_CONF]
    gate = rest_ref[0, :, 2 * D_LRU + D_CONF:]

    lwin_ref[0:LRU_HALO, :] = ltail_ref[...]
    lwin_ref[LRU_HALO:, :] = xr
    ltail_ref[...] = xr[t - LRU_HALO:, :]
    conv = lcb_ref[...]
    for k in range(LRU_CONV_WIDTH):
        off = LRU_HALO - (LRU_CONV_WIDTH - 1) + k
        conv = conv + lcw_ref[k:k + 1, :] * lwin_ref[off:off + t, :]
    gates = jnp.dot(conv.astype(BF16), wg_ref[...], preferred_element_type=F32)
    gate_r = gates[:, 0:D_LRU] + ba_ref[...]
    gate_i = gates[:, D_LRU:] + bx_ref[...]
    log_a = LRU_C * jax.nn.sigmoid(gate_r) * jax.nn.log_sigmoid(lam_ref[...])
    a = jnp.exp(log_a)
    th = jnp.tanh(log_a)
    u = jnp.sqrt(-2.0 * th / (1.0 - th)) * (jax.nn.sigmoid(gate_i) * conv)
    d = 1
    while d < t:
        u = a * _shift_rows(u, d, 0.0) + u
        a = a * _shift_rows(a, d, 1.0)
        d *= 2
    h = a * h_ref[...] + u
    h_ref[...] = h[t - 1:t, :]
    y_ref[0, :, 0:D_LRU] = h * jax.nn.gelu(xg)

    glu = val * jax.nn.sigmoid(gate)
    cwin_ref[0:CONF_HALO, :] = ctail_ref[...]
    cwin_ref[CONF_HALO:, :] = glu
    ctail_ref[...] = glu[t - CONF_HALO:, :]
    acc = cb_ref[...]
    first_off = CONF_HALO - (CONF_CONV_WIDTH - 1)
    for r in range(8):
        offs = [o for o in range(first_off, first_off + CONF_CONV_WIDTH) if o % 8 == r]
        span = t + (8 if r else 0)
        part = None
        for o in offs:
            term = cw_ref[o - first_off:o - first_off + 1, :] * cwin_ref[o - r:o - r + span, :]
            part = term if part is None else part + term
        acc = acc + part[r:r + t, :]
    mu = jnp.mean(acc, axis=-1, keepdims=True)
    cen = acc - mu
    var = jnp.mean(cen * cen, axis=-1, keepdims=True)
    ln = cen * lax.rsqrt(var + EPS) * lng_ref[...] + lnb_ref[...]
    sw = (ln * jax.nn.sigmoid(ln)).astype(BF16)
    y_ref[0, :, D_LRU:] = jnp.dot(sw, pw_ref[...], preferred_element_type=F32) + pb_ref[...]


def _mixers(rest, bsz, seq, lcw, lcb, wgates, ba, bx, lam, cw, cb, lng, lnb, pw, pb):
    t = MIX_TILE
    rest3 = rest.reshape(bsz, seq, D_REST)
    const = lambda b, i: (0, 0)
    row = lambda d: pl.BlockSpec((1, d), const)
    return pl.pallas_call(
        _mixers_kernel,
        grid=(bsz, seq // t),
        in_specs=[
            pl.BlockSpec((1, t, D_REST), lambda b, i: (b, i, 0)),
            pl.BlockSpec((LRU_CONV_WIDTH, D_LRU), const), row(D_LRU),
            pl.BlockSpec((D_LRU, 2 * D_LRU), const), row(D_LRU), row(D_LRU), row(D_LRU),
            pl.BlockSpec((CONF_CONV_WIDTH, D_CONF), const), row(D_CONF),
            row(D_CONF), row(D_CONF),
            pl.BlockSpec((D_CONF, D_CONF), const), row(D_CONF),
        ],
        out_specs=pl.BlockSpec((1, t, D_LRU + D_CONF), lambda b, i: (b, i, 0)),
        out_shape=jax.ShapeDtypeStruct((bsz, seq, D_LRU + D_CONF), F32),
        scratch_shapes=[
            pltpu.VMEM((LRU_HALO, D_LRU), F32),
            pltpu.VMEM((1, D_LRU), F32),
            pltpu.VMEM((CONF_HALO, D_CONF), F32),
            pltpu.VMEM((LRU_HALO + t, D_LRU), F32),
            pltpu.VMEM((CONF_HALO + t, D_CONF), F32),
        ],
        compiler_params=_cparams("arbitrary", "arbitrary"),
        name="mixers",
    )(rest3, lcw, lcb, wgates, ba, bx, lam, cw, cb, lng, lnb, pw, pb)


ROUTE_GATE_A, ROUTE_GATE_B, ROUTE_BUCKET, ROUTE_RANK = range(4)
ROUTE_ROWS = 8
PAIRS_PER_GROUP = EXPERTS_PER_GROUP * (EXPERTS_PER_GROUP - 1) // 2
N_BUCKETS = N_GROUPS * PAIRS_PER_GROUP
assert N_BUCKETS <= LANES
D_ROUTED = D_PACKED + LANES
ROUTER_FINE_COL0 = N_GROUPS


def _outproj_kernel(ya_ref, ybc_ref, x_ref, gout_ref, wout_ref, g2_ref, wr_ref, ltri_ref,
                    x1_ref, h2_ref, route_t_ref, counts_ref, cnt_ref):
    @pl.when(pl.program_id(0) == 0)
    def _():
        cnt_ref[...] = jnp.zeros_like(cnt_ref)

    y = jnp.concatenate(
        [_rms(ya_ref[...]), _rms(ybc_ref[:, 0:D_LRU]), _rms(ybc_ref[:, D_LRU:])], axis=1)
    y = (y * gout_ref[...]).astype(BF16)
    x1 = x_ref[...] + jnp.dot(y, wout_ref[...], preferred_element_type=F32)
    x1_ref[...] = x1
    h2 = (_rms(x1) * g2_ref[...]).astype(BF16)
    h2_ref[:, 0:D_PACKED] = _pack_bf16_pairs(h2)
    lf = jnp.dot(h2, wr_ref[...], preferred_element_type=F32)

    tm = lf.shape[0]
    col_i = lax.broadcasted_iota(jnp.int32, (tm, LANES), 1)
    col = col_i.astype(F32)
    neg = -jnp.inf
    big = float(LANES)
    is_c = col_i < N_GROUPS
    lc = jnp.where(is_c, lf, neg)
    mc = jnp.max(lc, axis=-1, keepdims=True)
    grp = jnp.min(jnp.where(lc == mc, col, big), axis=-1, keepdims=True)
    w_grp = 1.0 / jnp.sum(jnp.where(is_c, jnp.exp(lf - mc), 0.0), axis=-1, keepdims=True)
    lo = ROUTER_FINE_COL0 + EXPERTS_PER_GROUP * grp
    in_grp = jnp.logical_and(col >= lo, col < lo + EXPERTS_PER_GROUP)
    l1 = jnp.where(in_grp, lf, neg)
    v1 = jnp.max(l1, axis=-1, keepdims=True)
    i1 = jnp.min(jnp.where(l1 == v1, col, big), axis=-1, keepdims=True)
    sel1 = col == i1
    l2 = jnp.where(sel1, neg, l1)
    v2 = jnp.max(l2, axis=-1, keepdims=True)
    i2 = jnp.min(jnp.where(l2 == v2, col, big), axis=-1, keepdims=True)
    sel2 = col == i2
    e21 = jnp.exp(v2 - v1)
    g1 = w_grp / (1.0 + e21)
    g2 = w_grp * e21 / (1.0 + e21)

    swap = i2 < i1
    a = jnp.where(swap, i2, i1) - lo
    b = jnp.where(swap, i1, i2) - lo
    gate_a = jnp.where(swap, g2, g1)
    gate_b = jnp.where(swap, g1, g2)
    pair = a * (2 * EXPERTS_PER_GROUP - 1 - a) * 0.5 + (b - a - 1.0)
    bucket = grp * PAIRS_PER_GROUP + pair

    sel = col == bucket
    onehot = jnp.where(sel, 1.0, 0.0)
    before = jnp.dot(ltri_ref[...], onehot.astype(BF16), preferred_element_type=F32)
    cnt = cnt_ref[...]
    rank = jnp.sum(jnp.where(sel, before + cnt, 0.0), axis=-1, keepdims=True)
    cnt_ref[...] = cnt + jnp.sum(onehot, axis=0, keepdims=True)

    out = jnp.zeros((tm, LANES), F32)
    for lane_id, val in ((ROUTE_GATE_A, gate_a), (ROUTE_GATE_B, gate_b), (ROUTE_BUCKET, bucket), (ROUTE_RANK, rank)):
        out = jnp.where(col_i == lane_id, val, out)
    h2_ref[:, D_PACKED:] = pltpu.bitcast(out, jnp.uint32)
    route_t_ref[...] = out.T[0:ROUTE_ROWS, :]
    counts_ref[...] = cnt_ref[...]


def _outproj(ya, ybc, x, gout, wout, g2, wr, ltri):
    n = x.shape[0]
    tm = DENSE_TILE
    const = lambda i: (0, 0)
    tile = lambda d: pl.BlockSpec((tm, d), lambda i: (i, 0))
    return pl.pallas_call(
        _outproj_kernel,
        grid=(n // tm,),
        in_specs=[
            tile(D_SB), tile(D_LRU + D_CONF), tile(D_MODEL),
            pl.BlockSpec((1, D_MODEL), const),
            pl.BlockSpec((D_MODEL, D_MODEL), const),
            pl.BlockSpec((1, D_MODEL), const),
            pl.BlockSpec((D_MODEL, LANES), const),
            pl.BlockSpec((tm, tm), const),
        ],
        out_specs=[tile(D_MODEL), tile(D_ROUTED),
                   pl.BlockSpec((ROUTE_ROWS, tm), lambda i: (0, i)),
                   pl.BlockSpec((1, LANES), const)],
        out_shape=[
            jax.ShapeDtypeStruct((n, D_MODEL), F32),
            jax.ShapeDtypeStruct((n, D_ROUTED), jnp.uint32),
            jax.ShapeDtypeStruct((ROUTE_ROWS, n), F32),
            jax.ShapeDtypeStruct((1, LANES), F32),
        ],
        scratch_shapes=[pltpu.VMEM((1, LANES), F32)],
        compiler_params=_cparams("arbitrary"),
        name="outproj",
    )(ya, ybc, x, gout, wout, g2, wr, ltri)


def _row_copy(src_ref, src_row, dst_ref, dst_row, sem):
    return pltpu.make_async_copy(src_ref.at[pl.ds(src_row, 1)], dst_ref.at[pl.ds(dst_row, 1)], sem)


ZERO_CHUNK_ROWS = (128, 64, 32, 16, 8)


def _dispatch_kernel(pos_ref, padlo_ref, padhi_ref, nv_ref, h2_ref, xs_ref, zero_ref, sem, zsem):
    tm = ROW_TILE
    i = pl.program_id(0)
    base = i * tm

    @pl.when(i == 0)
    def _():
        zero_ref[...] = jnp.zeros_like(zero_ref)

        def fill_tail(action, e, carry):
            def go(copy):
                copy.start() if action == "start" else copy.wait()

            lo, hi = padlo_ref[e], padhi_ref[e]
            lo8 = jnp.minimum((lo + 7) // 8 * 8, hi)

            def one_row(r, c):
                go(_row_copy(zero_ref, 0, xs_ref, r, zsem))
                return c
            lax.fori_loop(lo, lo8, one_row, 0)
            groups = (hi - lo8) // 8
            cur = lo8
            for rows in ZERO_CHUNK_ROWS:
                take = (groups & (rows // 8)) != 0

                @pl.when(take)
                def _(rows=rows, cur=cur):
                    dst = xs_ref.at[pl.ds(pl.multiple_of(cur, 8), rows)]
                    go(pltpu.make_async_copy(zero_ref.at[pl.ds(0, rows)], dst, zsem))
                cur = cur + jnp.where(take, rows, 0)
            return carry

        def idle_block(action, b, carry):
            rows = pl.ds(pl.multiple_of(b * MOE_BLOCK, MOE_BLOCK), MOE_BLOCK)
            copy = pltpu.make_async_copy(zero_ref, xs_ref.at[rows], zsem)
            copy.start() if action == "start" else copy.wait()
            return carry

        n_blocks = xs_ref.shape[0] // MOE_BLOCK
        for action in ("start", "wait"):
            lax.fori_loop(0, N_BUCKETS, functools.partial(fill_tail, action), 0)
            lax.fori_loop(nv_ref[0], n_blocks, functools.partial(idle_block, action), 0)

    def start(r, c):
        _row_copy(h2_ref, r, xs_ref, pos_ref[base + r], sem).start()
        return c
    lax.fori_loop(0, tm, start, 0, unroll=DMA_UNROLL)

    def wait(r, c):
        _row_copy(h2_ref, r, xs_ref, pos_ref[base + r], sem).wait()
        return c
    lax.fori_loop(0, tm, wait, 0, unroll=DMA_UNROLL)


def _dispatch(pos, padlo, padhi, n_valid, h2, n_slots):
    n = h2.shape[0]
    tm = ROW_TILE
    assert sum(ZERO_CHUNK_ROWS) + 8 == MOE_BLOCK
    return pl.pallas_call(
        _dispatch_kernel,
        grid_spec=pltpu.PrefetchScalarGridSpec(
            num_scalar_prefetch=4,
            grid=(n // tm,),
            in_specs=[pl.BlockSpec((tm, D_ROUTED), lambda i, *_: (i, 0))],
            out_specs=pl.BlockSpec(memory_space=pl.ANY),
            scratch_shapes=[
                pltpu.VMEM((MOE_BLOCK, D_ROUTED), jnp.uint32),
                pltpu.SemaphoreType.DMA(()),
                pltpu.SemaphoreType.DMA(()),
            ],
        ),
        out_shape=jax.ShapeDtypeStruct((n_slots, D_ROUTED), jnp.uint32),
        compiler_params=_cparams("arbitrary"),
        name="dispatch",
    )(pos, padlo, padhi, n_valid, h2)


def _experts_kernel(ea_ref, eb_ref, nv_ref, xs_ref, wg_hbm, wu_hbm, wd_hbm, ys_ref,
                    wg_bf, wu_bf, wd_bf, wg_stage, wu_stage, wd_stage, sems, *, layer):
    i = pl.program_id(0)
    live = i < nv_ref[0]
    group = ea_ref[i] // EXPERTS_PER_GROUP
    new_group = jnp.logical_or(i == 0, group != ea_ref[jnp.maximum(i - 1, 0)] // EXPERTS_PER_GROUP)
    hbm_stage_cache = ((wg_hbm, wg_stage, wg_bf), (wu_hbm, wu_stage, wu_bf), (wd_hbm, wd_stage, wd_bf))

    @pl.when(jnp.logical_and(live, new_group))
    def _():
        def fetch(j, action):
            for k, (hbm, stage, _) in enumerate(hbm_stage_cache):
                copy = pltpu.make_async_copy(hbm.at[layer, group * EXPERTS_PER_GROUP + j],
                                             stage.at[j % 2], sems.at[j % 2, k])
                copy.start() if action == "start" else copy.wait()

        fetch(0, "start")
        for j in range(EXPERTS_PER_GROUP):
            if j + 1 < EXPERTS_PER_GROUP:
                fetch(j + 1, "start")
            fetch(j, "wait")
            for _, stage, cache in hbm_stage_cache:
                cache[j] = stage[j % 2].astype(BF16)

    @pl.when(live)
    def _():
        x = _unpack_bf16_pairs(xs_ref[:, 0:D_PACKED])
        record = pltpu.bitcast(xs_ref[:, D_PACKED:], F32)
        lane = lax.broadcasted_iota(jnp.int32, record.shape, 1)
        y = None
        for gate_lane, e_ref in ((ROUTE_GATE_A, ea_ref), (ROUTE_GATE_B, eb_ref)):
            e = e_ref[i] % EXPERTS_PER_GROUP
            gate = jnp.sum(jnp.where(lane == gate_lane, record, 0.0), axis=-1, keepdims=True)
            g = jnp.dot(x, wg_bf[e], preferred_element_type=F32)
            u = jnp.dot(x, wu_bf[e], preferred_element_type=F32)
            act = (g * jax.nn.sigmoid(g) * u).astype(BF16)
            part = jnp.dot(act, wd_bf[e], preferred_element_type=F32) * gate
            y = part if y is None else y + part
        ys_ref[...] = y

    @pl.when(jnp.logical_not(live))
    def _():
        ys_ref[...] = jnp.zeros_like(ys_ref)


def _experts(block_ea, block_eb, n_valid, xs, wg, wu, wd, layer):
    n_slots = xs.shape[0]
    n_blocks = n_slots // MOE_BLOCK
    hbm = pl.BlockSpec(memory_space=pl.ANY)
    return pl.pallas_call(
        functools.partial(_experts_kernel, layer=layer),
        grid_spec=pltpu.PrefetchScalarGridSpec(
            num_scalar_prefetch=3,
            grid=(n_blocks,),
            in_specs=[
                pl.BlockSpec((MOE_BLOCK, D_ROUTED), lambda i, ea, eb, nv: (jnp.minimum(i, nv[0] - 1), 0)),
                hbm, hbm, hbm,
            ],
            out_specs=pl.BlockSpec((MOE_BLOCK, D_MODEL), lambda i, ea, eb, nv: (i, 0)),
            scratch_shapes=[
                pltpu.VMEM((EXPERTS_PER_GROUP, D_MODEL, D_EXPERT), BF16),
                pltpu.VMEM((EXPERTS_PER_GROUP, D_MODEL, D_EXPERT), BF16),
                pltpu.VMEM((EXPERTS_PER_GROUP, D_EXPERT, D_MODEL), BF16),
                pltpu.VMEM((2, D_MODEL, D_EXPERT), F32),
                pltpu.VMEM((2, D_MODEL, D_EXPERT), F32),
                pltpu.VMEM((2, D_EXPERT, D_MODEL), F32),
                pltpu.SemaphoreType.DMA((2, 3)),
            ],
        ),
        out_shape=jax.ShapeDtypeStruct((n_slots, D_MODEL), F32),
        compiler_params=pltpu.CompilerParams(dimension_semantics=("arbitrary",),
                                             vmem_limit_bytes=EXPERTS_VMEM_LIMIT),
        name="experts",
    )(block_ea, block_eb, n_valid, xs, wg, wu, wd)


def _combine_kernel(pos_ref, ys_ref, x1_ref, out_ref, buf_ref, sems):
    tm = ROW_TILE
    i = pl.program_id(0)
    n_steps = pl.num_programs(0)

    def gather(step, slot, action):
        base = step * tm

        def body(r, c):
            cp = pltpu.make_async_copy(ys_ref.at[pl.ds(pos_ref[base + r], 1)],
                                       buf_ref.at[slot, pl.ds(r, 1)], sems.at[slot])
            cp.start() if action == "start" else cp.wait()
            return c
        lax.fori_loop(0, tm, body, 0, unroll=DMA_UNROLL)

    slot = lax.rem(i, 2)

    @pl.when(i == 0)
    def _():
        gather(0, 0, "start")

    @pl.when(i + 1 < n_steps)
    def _():
        gather(i + 1, 1 - slot, "start")

    gather(i, slot, "wait")
    out_ref[...] = x1_ref[...] + buf_ref[slot]


def _combine(pos, ys, x1):
    n = x1.shape[0]
    tm = ROW_TILE
    return pl.pallas_call(
        _combine_kernel,
        grid_spec=pltpu.PrefetchScalarGridSpec(
            num_scalar_prefetch=1,
            grid=(n // tm,),
            in_specs=[
                pl.BlockSpec(memory_space=pl.ANY),
                pl.BlockSpec((tm, D_MODEL), lambda i, *_: (i, 0)),
            ],
            out_specs=pl.BlockSpec((tm, D_MODEL), lambda i, *_: (i, 0)),
            scratch_shapes=[
                pltpu.VMEM((2, tm, D_MODEL), F32),
                pltpu.SemaphoreType.DMA((2,)),
            ],
        ),
        out_shape=jax.ShapeDtypeStruct((n, D_MODEL), F32),
        compiler_params=_cparams("arbitrary"),
        name="combine",
    )(pos, ys, x1)


def _bucket_experts():
    ea, eb = [], []
    for g in range(N_GROUPS):
        for a in range(EXPERTS_PER_GROUP):
            for b in range(a + 1, EXPERTS_PER_GROUP):
                ea.append(g * EXPERTS_PER_GROUP + a)
                eb.append(g * EXPERTS_PER_GROUP + b)
    return jnp.asarray(ea, jnp.int32), jnp.asarray(eb, jnp.int32)


def _slot_plan(route_t, counts_row, n_tok):
    bucket = route_t[ROUTE_BUCKET].astype(jnp.int32)
    rank = route_t[ROUTE_RANK].astype(jnp.int32)
    buckets = jnp.arange(N_BUCKETS, dtype=jnp.int32)
    counts = counts_row[0, 0:N_BUCKETS].astype(jnp.int32)
    padded = (counts + MOE_BLOCK - 1) // MOE_BLOCK * MOE_BLOCK
    padded_ends = jnp.cumsum(padded)
    padded_starts = padded_ends - padded
    pos = jnp.sum(jnp.where(bucket[None, :] == buckets[:, None], padded_starts[:, None], 0), axis=0) + rank
    n_blocks = -(-n_tok // MOE_BLOCK) + N_BUCKETS
    n_valid = (padded_ends[-1] // MOE_BLOCK).astype(jnp.int32)
    block_start = jnp.arange(n_blocks, dtype=jnp.int32) * MOE_BLOCK
    block_start = jnp.minimum(block_start, padded_ends[-1] - MOE_BLOCK)
    block_bucket = jnp.sum((padded_ends[None, :] <= block_start[:, None]).astype(jnp.int32), axis=1)
    block_bucket = jnp.minimum(block_bucket, N_BUCKETS - 1)
    in_bucket = block_bucket[:, None] == buckets[None, :]
    ea, eb = _bucket_experts()
    return dict(pos=pos, padlo=(padded_starts + counts).astype(jnp.int32), padhi=padded_ends.astype(jnp.int32),
                block_ea=jnp.sum(jnp.where(in_bucket, ea[None, :], 0), axis=1),
                block_eb=jnp.sum(jnp.where(in_bucket, eb[None, :], 0), axis=1),
                n_valid=n_valid.reshape(1), n_slots=n_blocks * MOE_BLOCK)


def _block_diag(w):
    nb, r, c = w.shape
    eye = jnp.eye(nb, dtype=w.dtype)
    return (w[:, :, None, :] * eye[:, None, :, None]).reshape(nb * r, nb * c)


def kernel(x, norm1_g, w_in, q_norm_g, k_norm_g, lru_conv_w, lru_conv_b, lru_wa, lru_ba, lru_wx, lru_bx, lru_lambda, conf_dw_w, conf_dw_b, conf_ln_g, conf_ln_b, conf_pw_w, conf_pw_b, out_norm_g, w_out, norm2_g, router_coarse, router_fine, exp_w_gate, exp_w_up, exp_w_down):
    bsz, seq, d = x.shape
    n = bsz * seq
    depth = w_in.shape[0]
    assert d == D_MODEL and n % DENSE_TILE == 0 and n % ROW_TILE == 0
    assert seq % MIX_TILE == 0 and seq % (ATTN_TILES * ATTN_BQ) == 0

    heads = D_SB // HEAD_DIM
    head_of = jnp.arange(D_SB, dtype=jnp.int32) // HEAD_DIM
    gsum = (head_of[:, None] == head_of[None, :]).astype(BF16)
    t_io = jnp.arange(DENSE_TILE, dtype=jnp.int32)
    ltri = (t_io[None, :] < t_io[:, None]).astype(BF16)
    row = lambda v: v.reshape(1, -1)

    xf = x.reshape(n, d)
    for l in range(depth):
        qkv, rest = _inproj(xf, row(norm1_g[l]), w_in[l].astype(BF16),
                            row(jnp.tile(q_norm_g[l], heads)), row(jnp.tile(k_norm_g[l], heads)), gsum)
        ya = _attention(qkv, bsz, seq).reshape(n, D_SB)
        wgates = jnp.concatenate([_block_diag(lru_wa[l]), _block_diag(lru_wx[l])], axis=1).astype(BF16)
        ybc = _mixers(rest, bsz, seq, lru_conv_w[l], row(lru_conv_b[l]), wgates,
                      row(lru_ba[l]), row(lru_bx[l]), row(lru_lambda[l]),
                      conf_dw_w[l], row(conf_dw_b[l]), row(conf_ln_g[l]), row(conf_ln_b[l]),
                      conf_pw_w[l].astype(BF16), row(conf_pw_b[l])).reshape(n, D_LRU + D_CONF)
        wr = jnp.concatenate([router_coarse[l], router_fine[l]], axis=1)
        wr = jnp.pad(wr, ((0, 0), (0, LANES - wr.shape[1]))).astype(BF16)
        x1, h2, route_t, counts = _outproj(ya, ybc, xf, row(out_norm_g[l]), w_out[l].astype(BF16),
                                           row(norm2_g[l]), wr, ltri)
        plan = _slot_plan(route_t, counts, n)
        xs = _dispatch(plan["pos"], plan["padlo"], plan["padhi"], plan["n_valid"], h2, plan["n_slots"])
        ys = _experts(plan["block_ea"], plan["block_eb"], plan["n_valid"], xs,
                      exp_w_gate, exp_w_up, exp_w_down, l)
        xf = _combine(plan["pos"], ys, x1)
    return xf.reshape(bsz, seq, d)
```

```python
import functools

import jax
import jax.numpy as jnp
from jax import lax
from jax.experimental import pallas as pl
from jax.experimental.pallas import tpu as pltpu

F32 = jnp.float32
BF16 = jnp.bfloat16

D_MODEL = 1024
HEAD_DIM = 64
D_SB = 512
D_LRU = 256
LRU_BLOCKS = 4
LRU_CONV_WIDTH = 4
LRU_C = 8.0
D_CONF = 256
CONF_CONV_WIDTH = 31
D_QKV = 3 * D_SB
D_REST = 2 * D_LRU + 2 * D_CONF
N_GROUPS = 4
EXPERTS_PER_GROUP = 8
N_EXPERTS = N_GROUPS * EXPERTS_PER_GROUP
D_EXPERT = 512
D_PACKED = D_MODEL // 2
MOE_BLOCK = 256
EPS = 1e-6

LANES = 128
ROW_TILE = 1024
DENSE_TILE = 512
ATTN_BQ = 256
ATTN_BK = 128
ATTN_TILES = 8
MIX_TILE = 512
CONF_HALO = 32
DMA_UNROLL = 8
LRU_HALO = 8
ATTN_DEAD_LOG = -110.0
VMEM_LIMIT = 48 * 1024 * 1024
EXPERTS_VMEM_LIMIT = 56 * 1024 * 1024


def _cparams(*sem):
    return pltpu.CompilerParams(dimension_semantics=sem, vmem_limit_bytes=VMEM_LIMIT)


def _rms(y):
    return y * lax.rsqrt(jnp.mean(y * y, axis=-1, keepdims=True) + EPS)


def _pack_bf16_pairs(x):
    c = x.shape[1] // 2
    bits = pltpu.bitcast(x.astype(F32), jnp.uint32)
    return (bits[:, :c] >> 16) | (bits[:, c:] & jnp.uint32(0xFFFF0000))


def _unpack_bf16_pairs(words):
    low = pltpu.bitcast(words << 16, F32)
    high = pltpu.bitcast(words & jnp.uint32(0xFFFF0000), F32)
    return jnp.concatenate([low, high], axis=1).astype(BF16)


def _inproj_kernel(x_ref, g_ref, w_ref, qg_ref, kg_ref, gsum_ref, qkv_ref, rest_ref):
    h = (_rms(x_ref[...]) * g_ref[...]).astype(BF16)

    def head_norm(p, gain):
        s = jnp.dot((p * p).astype(BF16), gsum_ref[...], preferred_element_type=F32)
        return p * lax.rsqrt(s * (1.0 / HEAD_DIM) + EPS) * gain

    q = jnp.dot(h, w_ref[:, 0:D_SB], preferred_element_type=F32)
    qkv_ref[:, 0:D_SB] = head_norm(q, qg_ref[...]).astype(BF16)
    k = jnp.dot(h, w_ref[:, D_SB:2 * D_SB], preferred_element_type=F32)
    qkv_ref[:, D_SB:2 * D_SB] = head_norm(k, kg_ref[...]).astype(BF16)
    v = jnp.dot(h, w_ref[:, 2 * D_SB:D_QKV], preferred_element_type=F32)
    qkv_ref[:, 2 * D_SB:D_QKV] = v.astype(BF16)
    rest_ref[...] = jnp.dot(h, w_ref[:, D_QKV:], preferred_element_type=F32)


def _inproj(x, g, w, qg, kg, gsum):
    n = x.shape[0]
    tm = DENSE_TILE
    const = lambda i: (0, 0)
    return pl.pallas_call(
        _inproj_kernel,
        grid=(n // tm,),
        in_specs=[
            pl.BlockSpec((tm, D_MODEL), lambda i: (i, 0)),
            pl.BlockSpec((1, D_MODEL), const),
            pl.BlockSpec((D_MODEL, D_QKV + D_REST), const),
            pl.BlockSpec((1, D_SB), const),
            pl.BlockSpec((1, D_SB), const),
            pl.BlockSpec((D_SB, D_SB), const),
        ],
        out_specs=[
            pl.BlockSpec((tm, D_QKV), lambda i: (i, 0)),
            pl.BlockSpec((tm, D_REST), lambda i: (i, 0)),
        ],
        out_shape=[
            jax.ShapeDtypeStruct((n, D_QKV), BF16),
            jax.ShapeDtypeStruct((n, D_REST), F32),
        ],
        compiler_params=_cparams("arbitrary"),
        name="inproj",
    )(x, g, w, qg, kg, gsum)


def _attn_kernel(q_ref, k_ref, v_ref, o_ref, *scratch):
    def tile(tt, carry):
        rows = pl.ds(pl.multiple_of(tt * ATTN_BQ, ATTN_BQ), ATTN_BQ)
        _attn_tile(pl.program_id(2) * ATTN_TILES + tt, q_ref.at[0, rows], k_ref, v_ref, o_ref.at[0, rows], *scratch)
        return carry
    lax.fori_loop(0, ATTN_TILES, tile, 0)


def _attn_tile(i, q_ref, k_ref, v_ref, o_ref, z0, z1, lbc0, lbc1, hl0, hl1, c0, c1, cm0, cm1, w0, w1, acc_ref):
    bq, bk = ATTN_BQ, ATTN_BK
    heads = (0, 1)
    z_refs, lbc_refs, hilo_refs, c_refs, cmax_refs = (z0, z1), (lbc0, lbc1), (hl0, hl1), (c0, c1), (cm0, cm1)
    w_refs = (w0, w1)
    q = q_ref[...] * jnp.asarray(HEAD_DIM ** -0.5, BF16)
    first = lax.broadcasted_iota(jnp.int32, (bk, LANES), 1) < HEAD_DIM
    t_pos = lax.broadcasted_iota(jnp.int32, (bq, bk), 0) + i * bq
    s_io = lax.broadcasted_iota(jnp.int32, (bq, bk), 1)
    earlier = (lax.broadcasted_iota(jnp.int32, (bk, bk), 0)
               > lax.broadcasted_iota(jnp.int32, (bk, bk), 1))
    neg_later = jnp.where(earlier, -1.0, 0.0).astype(BF16)
    neg_later2 = jnp.concatenate([neg_later, neg_later], axis=0)

    def rows(m):
        return pl.ds(pl.multiple_of(jnp.maximum(m, 0) * bk, bk), bk)

    def causal(m):
        return (s_io + m * bk) < t_pos

    def per_head(x):
        zero = jnp.zeros_like(x)
        return jnp.concatenate([jnp.where(first, x, zero), jnp.where(first, zero, x)], axis=0)

    def stage0(m, s):
        z = lax.dot_general(q, per_head(k_ref[0, rows(m), :]), (((1,), (1,)), ((), ())),
                            preferred_element_type=F32)
        for h in heads:
            z_refs[s][h] = z[:, h * bk:(h + 1) * bk]

    def stage1(m, s, masked):
        mask = causal(m) if masked else None
        for h in heads:
            z = z_refs[s][h]
            c = c_refs[s][h]
            drop = jnp.maximum(z, 0.0) + jnp.log(1.0 + jnp.exp(-jnp.abs(z)))
            lbc_refs[s][h] = (z - drop) + c
            if masked:
                drop = jnp.where(mask, drop, 0.0)
            hi = drop.astype(BF16)
            lo = (drop - hi.astype(F32)).astype(BF16)
            hilo_refs[s][h] = jnp.concatenate([hi, lo], axis=1)
            c_next = c - jnp.sum(drop, axis=-1, keepdims=True)
            c_refs[1 - s][h] = c_next
            cmax_refs[1 - s][h] = jnp.max(c_next.reshape(bq // 8, 8, LANES), axis=0)

    def stage2_sums(s):
        return [jnp.dot(hilo_refs[s][h], neg_later2, preferred_element_type=F32) for h in heads]

    def stage2_apply(m, s, within, masked):
        mask = causal(m) if masked else None
        ws = []
        for h in heads:
            w = jnp.exp(lbc_refs[s][h] + within[h])
            if masked:
                w = jnp.where(mask, w, 0.0)
            ws.append(w.astype(BF16))
        w_refs[s][...] = jnp.concatenate(ws, axis=1)

    def stage3(m, s):
        acc_ref[...] += jnp.dot(w_refs[s][...], per_head(v_ref[0, rows(m), :]), preferred_element_type=F32)

    def alive(m, s):
        return jnp.logical_and(m >= 0, jnp.max(cmax_refs[s][...]) > ATTN_DEAD_LOG)

    top = (i + 1) * (bq // bk) - 1
    acc_ref[...] = jnp.zeros_like(acc_ref)
    c_refs[1][...] = jnp.zeros((2, bq, LANES), F32)

    def step(m, s, masked_next=False, masked=False, has_prev=True):
        live_next = alive(m - 1, 1 - s)
        if has_prev:
            stage3(m + 1, 1 - s)
        within = stage2_sums(s)
        stage0(m - 2, s)
        stage1(m - 1, 1 - s, masked_next)
        stage2_apply(m, s, within, masked)
        return live_next

    stage0(top, 1)
    stage0(top - 1, 0)
    stage1(top, 1, True)
    step(top, 1, masked_next=True, masked=True, has_prev=False)
    step(top - 1, 0, masked=True)

    def body(st):
        m, _ = st
        live_next = lax.cond((m & 1) == 0, functools.partial(step, m, 0), functools.partial(step, m, 1))
        return m - 1, live_next

    m_end, _ = lax.while_loop(lambda st: st[1], body, (top - 2, alive(top - 2, 1)))
    for s in (0, 1):
        pl.when(((m_end + 1) & 1) == s)(functools.partial(stage3, m_end + 1, s))
    o_ref[...] = acc_ref[...]


def _attention(qkv, bsz, seq):
    bq, bk = ATTN_BQ, ATTN_BK
    assert bq == 2 * bk
    pairs = D_SB // LANES
    qkv3 = qkv.reshape(bsz, seq, D_QKV)
    step_rows = ATTN_TILES * bq
    return pl.pallas_call(
        _attn_kernel,
        grid=(bsz, pairs, seq // step_rows),
        in_specs=[
            pl.BlockSpec((1, step_rows, LANES), lambda b, p, i: (b, i, p)),
            pl.BlockSpec((1, seq, LANES), lambda b, p, i: (b, 0, pairs + p)),
            pl.BlockSpec((1, seq, LANES), lambda b, p, i: (b, 0, 2 * pairs + p)),
        ],
        out_specs=pl.BlockSpec((1, step_rows, LANES), lambda b, p, i: (b, i, p)),
        out_shape=jax.ShapeDtypeStruct((bsz, seq, D_SB), F32),
        scratch_shapes=(
            [pltpu.VMEM((2, bq, bk), F32)] * 2
            + [pltpu.VMEM((2, bq, bk), F32)] * 2
            + [pltpu.VMEM((2, bq, 2 * bk), BF16)] * 2
            + [pltpu.VMEM((2, bq, LANES), F32)] * 2
            + [pltpu.VMEM((2, 8, LANES), F32)] * 2
            + [pltpu.VMEM((bq, 2 * bk), BF16)] * 2
            + [pltpu.VMEM((bq, LANES), F32)]
        ),
        compiler_params=_cparams("arbitrary", "arbitrary", "arbitrary"),
        name="attn",
    )(qkv3, qkv3, qkv3)


def _shift_rows(x, d, fill):
    t = x.shape[0]
    if d % 8 == 0:
        return jnp.concatenate([jnp.full((d, x.shape[1]), fill, x.dtype), x[:t - d]], axis=0)
    rows = lax.broadcasted_iota(jnp.int32, x.shape, 0)
    return jnp.where(rows < d, fill, pltpu.roll(x, d, axis=0))


def _mixers_kernel(rest_ref, lcw_ref, lcb_ref, wg_ref, ba_ref, bx_ref, lam_ref,
                   cw_ref, cb_ref, lng_ref, lnb_ref, pw_ref, pb_ref,
                   y_ref, ltail_ref, h_ref, ctail_ref, lwin_ref, cwin_ref):
    t = MIX_TILE

    @pl.when(pl.program_id(1) == 0)
    def _():
        ltail_ref[...] = jnp.zeros_like(ltail_ref)
        h_ref[...] = jnp.zeros_like(h_ref)
        ctail_ref[...] = jnp.zeros_like(ctail_ref)

    xr = rest_ref[0, :, 0:D_LRU]
    xg = rest_ref[0, :, D_LRU:2 * D_LRU]
    val = rest_ref[0, :, 2 * D_LRU:2 * D_LRU + D_CONF]
    gate = rest_ref[0, :, 2 * D_LRU + D_CONF:]

    lwin_ref[0:LRU_HALO, :] = ltail_ref[...]
    lwin_ref[LRU_HALO:, :] = xr
    ltail_ref[...] = xr[t - LRU_HALO:, :]
    conv = lcb_ref[...]
    for k in range(LRU_CONV_WIDTH):
        off = LRU_HALO - (LRU_CONV_WIDTH - 1) + k
        conv = conv + lcw_ref[k:k + 1, :] * lwin_ref[off:off + t, :]
    gates = jnp.dot(conv.astype(BF16), wg_ref[...], preferred_element_type=F32)
    gate_r = gates[:, 0:D_LRU] + ba_ref[...]
    gate_i = gates[:, D_LRU:] + bx_ref[...]
    log_a = LRU_C * jax.nn.sigmoid(gate_r) * jax.nn.log_sigmoid(lam_ref[...])
    a = jnp.exp(log_a)
    th = jnp.tanh(log_a)
    u = jnp.sqrt(-2.0 * th / (1.0 - th)) * (jax.nn.sigmoid(gate_i) * conv)
    d = 1
    while d < t:
        u = a * _shift_rows(u, d, 0.0) + u
        a = a * _shift_rows(a, d, 1.0)
        d *= 2
    h = a * h_ref[...] + u
    h_ref[...] = h[t - 1:t, :]
    y_ref[0, :, 0:D_LRU] = h * jax.nn.gelu(xg)

    glu = val * jax.nn.sigmoid(gate)
    cwin_ref[0:CONF_HALO, :] = ctail_ref[...]
    cwin_ref[CONF_HALO:, :] = glu
    ctail_ref[...] = glu[t - CONF_HALO:, :]
    acc = cb_ref[...]
    first_off = CONF_HALO - (CONF_CONV_WIDTH - 1)
    for r in range(8):
        offs = [o for o in range(first_off, first_off + CONF_CONV_WIDTH) if o % 8 == r]
        span = t + (8 if r else 0)
        part = None
        for o in offs:
            term = cw_ref[o - first_off:o - first_off + 1, :] * cwin_ref[o - r:o - r + span, :]
            part = term if part is None else part + term
        acc = acc + part[r:r + t, :]
    mu = jnp.mean(acc, axis=-1, keepdims=True)
    cen = acc - mu
    var = jnp.mean(cen * cen, axis=-1, keepdims=True)
    ln = cen * lax.rsqrt(var + EPS) * lng_ref[...] + lnb_ref[...]
    sw = (ln * jax.nn.sigmoid(ln)).astype(BF16)
    y_ref[0, :, D_LRU:] = jnp.dot(sw, pw_ref[...], preferred_element_type=F32) + pb_ref[...]


def _mixers(rest, bsz, seq, lcw, lcb, wgates, ba, bx, lam, cw, cb, lng, lnb, pw, pb):
    t = MIX_TILE
    rest3 = rest.reshape(bsz, seq, D_REST)
    const = lambda b, i: (0, 0)
    row = lambda d: pl.BlockSpec((1, d), const)
    return pl.pallas_call(
        _mixers_kernel,
        grid=(bsz, seq // t),
        in_specs=[
            pl.BlockSpec((1, t, D_REST), lambda b, i: (b, i, 0)),
            pl.BlockSpec((LRU_CONV_WIDTH, D_LRU), const), row(D_LRU),
            pl.BlockSpec((D_LRU, 2 * D_LRU), const), row(D_LRU), row(D_LRU), row(D_LRU),
            pl.BlockSpec((CONF_CONV_WIDTH, D_CONF), const), row(D_CONF),
            row(D_CONF), row(D_CONF),
            pl.BlockSpec((D_CONF, D_CONF), const), row(D_CONF),
        ],
        out_specs=pl.BlockSpec((1, t, D_LRU + D_CONF), lambda b, i: (b, i, 0)),
        out_shape=jax.ShapeDtypeStruct((bsz, seq, D_LRU + D_CONF), F32),
        scratch_shapes=[
            pltpu.VMEM((LRU_HALO, D_LRU), F32),
            pltpu.VMEM((1, D_LRU), F32),
            pltpu.VMEM((CONF_HALO, D_CONF), F32),
            pltpu.VMEM((LRU_HALO + t, D_LRU), F32),
            pltpu.VMEM((CONF_HALO + t, D_CONF), F32),
        ],
        compiler_params=_cparams("arbitrary", "arbitrary"),
        name="mixers",
    )(rest3, lcw, lcb, wgates, ba, bx, lam, cw, cb, lng, lnb, pw, pb)


ROUTE_GATE_A, ROUTE_GATE_B, ROUTE_BUCKET, ROUTE_RANK = range(4)
ROUTE_ROWS = 8
PAIRS_PER_GROUP = EXPERTS_PER_GROUP * (EXPERTS_PER_GROUP - 1) // 2
N_BUCKETS = N_GROUPS * PAIRS_PER_GROUP
assert N_BUCKETS <= LANES
D_ROUTED = D_PACKED + LANES
ROUTER_FINE_COL0 = N_GROUPS


def _outproj_kernel(ya_ref, ybc_ref, x_ref, gout_ref, wout_ref, g2_ref, wr_ref, ltri_ref,
                    x1_ref, h2_ref, route_t_ref, counts_ref, cnt_ref):
    @pl.when(pl.program_id(0) == 0)
    def _():
        cnt_ref[...] = jnp.zeros_like(cnt_ref)

    y = jnp.concatenate(
        [_rms(ya_ref[...]), _rms(ybc_ref[:, 0:D_LRU]), _rms(ybc_ref[:, D_LRU:])], axis=1)
    y = (y * gout_ref[...]).astype(BF16)
    x1 = x_ref[...] + jnp.dot(y, wout_ref[...], preferred_element_type=F32)
    x1_ref[...] = x1
    h2 = (_rms(x1) * g2_ref[...]).astype(BF16)
    h2_ref[:, 0:D_PACKED] = _pack_bf16_pairs(h2)
    lf = jnp.dot(h2, wr_ref[...], preferred_element_type=F32)

    tm = lf.shape[0]
    col_i = lax.broadcasted_iota(jnp.int32, (tm, LANES), 1)
    col = col_i.astype(F32)
    neg = -jnp.inf
    big = float(LANES)
    is_c = col_i < N_GROUPS
    lc = jnp.where(is_c, lf, neg)
    mc = jnp.max(lc, axis=-1, keepdims=True)
    grp = jnp.min(jnp.where(lc == mc, col, big), axis=-1, keepdims=True)
    w_grp = 1.0 / jnp.sum(jnp.where(is_c, jnp.exp(lf - mc), 0.0), axis=-1, keepdims=True)
    lo = ROUTER_FINE_COL0 + EXPERTS_PER_GROUP * grp
    in_grp = jnp.logical_and(col >= lo, col < lo + EXPERTS_PER_GROUP)
    l1 = jnp.where(in_grp, lf, neg)
    v1 = jnp.max(l1, axis=-1, keepdims=True)
    i1 = jnp.min(jnp.where(l1 == v1, col, big), axis=-1, keepdims=True)
    sel1 = col == i1
    l2 = jnp.where(sel1, neg, l1)
    v2 = jnp.max(l2, axis=-1, keepdims=True)
    i2 = jnp.min(jnp.where(l2 == v2, col, big), axis=-1, keepdims=True)
    sel2 = col == i2
    e21 = jnp.exp(v2 - v1)
    g1 = w_grp / (1.0 + e21)
    g2 = w_grp * e21 / (1.0 + e21)

    swap = i2 < i1
    a = jnp.where(swap, i2, i1) - lo
    b = jnp.where(swap, i1, i2) - lo
    gate_a = jnp.where(swap, g2, g1)
    gate_b = jnp.where(swap, g1, g2)
    pair = a * (2 * EXPERTS_PER_GROUP - 1 - a) * 0.5 + (b - a - 1.0)
    bucket = grp * PAIRS_PER_GROUP + pair

    sel = col == bucket
    onehot = jnp.where(sel, 1.0, 0.0)
    before = jnp.dot(ltri_ref[...], onehot.astype(BF16), preferred_element_type=F32)
    cnt = cnt_ref[...]
    rank = jnp.sum(jnp.where(sel, before + cnt, 0.0), axis=-1, keepdims=True)
    cnt_ref[...] = cnt + jnp.sum(onehot, axis=0, keepdims=True)

    out = jnp.zeros((tm, LANES), F32)
    for lane_id, val in ((ROUTE_GATE_A, gate_a), (ROUTE_GATE_B, gate_b), (ROUTE_BUCKET, bucket), (ROUTE_RANK, rank)):
        out = jnp.where(col_i == lane_id, val, out)
    h2_ref[:, D_PACKED:] = pltpu.bitcast(out, jnp.uint32)
    route_t_ref[...] = out.T[0:ROUTE_ROWS, :]
    counts_ref[...] = cnt_ref[...]


def _outproj(ya, ybc, x, gout, wout, g2, wr, ltri):
    n = x.shape[0]
    tm = DENSE_TILE
    const = lambda i: (0, 0)
    tile = lambda d: pl.BlockSpec((tm, d), lambda i: (i, 0))
    return pl.pallas_call(
        _outproj_kernel,
        grid=(n // tm,),
        in_specs=[
            tile(D_SB), tile(D_LRU + D_CONF), tile(D_MODEL),
            pl.BlockSpec((1, D_MODEL), const),
            pl.BlockSpec((D_MODEL, D_MODEL), const),
            pl.BlockSpec((1, D_MODEL), const),
            pl.BlockSpec((D_MODEL, LANES), const),
            pl.BlockSpec((tm, tm), const),
        ],
        out_specs=[tile(D_MODEL), tile(D_ROUTED),
                   pl.BlockSpec((ROUTE_ROWS, tm), lambda i: (0, i)),
                   pl.BlockSpec((1, LANES), const)],
        out_shape=[
            jax.ShapeDtypeStruct((n, D_MODEL), F32),
            jax.ShapeDtypeStruct((n, D_ROUTED), jnp.uint32),
            jax.ShapeDtypeStruct((ROUTE_ROWS, n), F32),
            jax.ShapeDtypeStruct((1, LANES), F32),
        ],
        scratch_shapes=[pltpu.VMEM((1, LANES), F32)],
        compiler_params=_cparams("arbitrary"),
        name="outproj",
    )(ya, ybc, x, gout, wout, g2, wr, ltri)


def _row_copy(src_ref, src_row, dst_ref, dst_row, sem):
    return pltpu.make_async_copy(src_ref.at[pl.ds(src_row, 1)], dst_ref.at[pl.ds(dst_row, 1)], sem)


ZERO_CHUNK_ROWS = (128, 64, 32, 16, 8)


def _dispatch_kernel(pos_ref, padlo_ref, padhi_ref, nv_ref, h2_ref, xs_ref, zero_ref, sem, zsem):
    tm = ROW_TILE
    i = pl.program_id(0)
    base = i * tm

    @pl.when(i == 0)
    def _():
        zero_ref[...] = jnp.zeros_like(zero_ref)

        def fill_tail(action, e, carry):
            def go(copy):
                copy.start() if action == "start" else copy.wait()

            lo, hi = padlo_ref[e], padhi_ref[e]
            lo8 = jnp.minimum((lo + 7) // 8 * 8, hi)

            def one_row(r, c):
                go(_row_copy(zero_ref, 0, xs_ref, r, zsem))
                return c
            lax.fori_loop(lo, lo8, one_row, 0)
            groups = (hi - lo8) // 8
            cur = lo8
            for rows in ZERO_CHUNK_ROWS:
                take = (groups & (rows // 8)) != 0

                @pl.when(take)
                def _(rows=rows, cur=cur):
                    dst = xs_ref.at[pl.ds(pl.multiple_of(cur, 8), rows)]
                    go(pltpu.make_async_copy(zero_ref.at[pl.ds(0, rows)], dst, zsem))
                cur = cur + jnp.where(take, rows, 0)
            return carry

        def idle_block(action, b, carry):
            rows = pl.ds(pl.multiple_of(b * MOE_BLOCK, MOE_BLOCK), MOE_BLOCK)
            copy = pltpu.make_async_copy(zero_ref, xs_ref.at[rows], zsem)
            copy.start() if action == "start" else copy.wait()
            return carry

        n_blocks = xs_ref.shape[0] // MOE_BLOCK
        for action in ("start", "wait"):
            lax.fori_loop(0, N_BUCKETS, functools.partial(fill_tail, action), 0)
            lax.fori_loop(nv_ref[0], n_blocks, functools.partial(idle_block, action), 0)

    def start(r, c):
        _row_copy(h2_ref, r, xs_ref, pos_ref[base + r], sem).start()
        return c
    lax.fori_loop(0, tm, start, 0, unroll=DMA_UNROLL)

    def wait(r, c):
        _row_copy(h2_ref, r, xs_ref, pos_ref[base + r], sem).wait()
        return c
    lax.fori_loop(0, tm, wait, 0, unroll=DMA_UNROLL)


def _dispatch(pos, padlo, padhi, n_valid, h2, n_slots):
    n = h2.shape[0]
    tm = ROW_TILE
    assert sum(ZERO_CHUNK_ROWS) + 8 == MOE_BLOCK
    return pl.pallas_call(
        _dispatch_kernel,
        grid_spec=pltpu.PrefetchScalarGridSpec(
            num_scalar_prefetch=4,
            grid=(n // tm,),
            in_specs=[pl.BlockSpec((tm, D_ROUTED), lambda i, *_: (i, 0))],
            out_specs=pl.BlockSpec(memory_space=pl.ANY),
            scratch_shapes=[
                pltpu.VMEM((MOE_BLOCK, D_ROUTED), jnp.uint32),
                pltpu.SemaphoreType.DMA(()),
                pltpu.SemaphoreType.DMA(()),
            ],
        ),
        out_shape=jax.ShapeDtypeStruct((n_slots, D_ROUTED), jnp.uint32),
        compiler_params=_cparams("arbitrary"),
        name="dispatch",
    )(pos, padlo, padhi, n_valid, h2)


def _experts_kernel(ea_ref, eb_ref, nv_ref, xs_ref, wg_hbm, wu_hbm, wd_hbm, ys_ref,
                    wg_bf, wu_bf, wd_bf, wg_stage, wu_stage, wd_stage, sems, *, layer):
    i = pl.program_id(0)
    live = i < nv_ref[0]
    group = ea_ref[i] // EXPERTS_PER_GROUP
    new_group = jnp.logical_or(i == 0, group != ea_ref[jnp.maximum(i - 1, 0)] // EXPERTS_PER_GROUP)
    hbm_stage_cache = ((wg_hbm, wg_stage, wg_bf), (wu_hbm, wu_stage, wu_bf), (wd_hbm, wd_stage, wd_bf))

    @pl.when(jnp.logical_and(live, new_group))
    def _():
        def fetch(j, action):
            for k, (hbm, stage, _) in enumerate(hbm_stage_cache):
                copy = pltpu.make_async_copy(hbm.at[layer, group * EXPERTS_PER_GROUP + j],
                                             stage.at[j % 2], sems.at[j % 2, k])
                copy.start() if action == "start" else copy.wait()

        fetch(0, "start")
        for j in range(EXPERTS_PER_GROUP):
            if j + 1 < EXPERTS_PER_GROUP:
                fetch(j + 1, "start")
            fetch(j, "wait")
            for _, stage, cache in hbm_stage_cache:
                cache[j] = stage[j % 2].astype(BF16)

    @pl.when(live)
    def _():
        x = _unpack_bf16_pairs(xs_ref[:, 0:D_PACKED])
        record = pltpu.bitcast(xs_ref[:, D_PACKED:], F32)
        lane = lax.broadcasted_iota(jnp.int32, record.shape, 1)
        y = None
        for gate_lane, e_ref in ((ROUTE_GATE_A, ea_ref), (ROUTE_GATE_B, eb_ref)):
            e = e_ref[i] % EXPERTS_PER_GROUP
            gate = jnp.sum(jnp.where(lane == gate_lane, record, 0.0), axis=-1, keepdims=True)
            g = jnp.dot(x, wg_bf[e], preferred_element_type=F32)
            u = jnp.dot(x, wu_bf[e], preferred_element_type=F32)
            act = (g * jax.nn.sigmoid(g) * u).astype(BF16)
            part = jnp.dot(act, wd_bf[e], preferred_element_type=F32) * gate
            y = part if y is None else y + part
        ys_ref[...] = y

    @pl.when(jnp.logical_not(live))
    def _():
        ys_ref[...] = jnp.zeros_like(ys_ref)


def _experts(block_ea, block_eb, n_valid, xs, wg, wu, wd, layer):
    n_slots = xs.shape[0]
    n_blocks = n_slots // MOE_BLOCK
    hbm = pl.BlockSpec(memory_space=pl.ANY)
    return pl.pallas_call(
        functools.partial(_experts_kernel, layer=layer),
        grid_spec=pltpu.PrefetchScalarGridSpec(
            num_scalar_prefetch=3,
            grid=(n_blocks,),
            in_specs=[
                pl.BlockSpec((MOE_BLOCK, D_ROUTED), lambda i, ea, eb, nv: (jnp.minimum(i, nv[0] - 1), 0)),
                hbm, hbm, hbm,
            ],
            out_specs=pl.BlockSpec((MOE_BLOCK, D_MODEL), lambda i, ea, eb, nv: (i, 0)),
            scratch_shapes=[
                pltpu.VMEM((EXPERTS_PER_GROUP, D_MODEL, D_EXPERT), BF16),
                pltpu.VMEM((EXPERTS_PER_GROUP, D_MODEL, D_EXPERT), BF16),
                pltpu.VMEM((EXPERTS_PER_GROUP, D_EXPERT, D_MODEL), BF16),
                pltpu.VMEM((2, D_MODEL, D_EXPERT), F32),
                pltpu.VMEM((2, D_MODEL, D_EXPERT), F32),
                pltpu.VMEM((2, D_EXPERT, D_MODEL), F32),
                pltpu.SemaphoreType.DMA((2, 3)),
            ],
        ),
        out_shape=jax.ShapeDtypeStruct((n_slots, D_MODEL), F32),
        compiler_params=pltpu.CompilerParams(dimension_semantics=("arbitrary",),
                                             vmem_limit_bytes=EXPERTS_VMEM_LIMIT),
        name="experts",
    )(block_ea, block_eb, n_valid, xs, wg, wu, wd)


def _combine_kernel(pos_ref, ys_ref, x1_ref, out_ref, buf_ref, sems):
    tm = ROW_TILE
    i = pl.program_id(0)
    n_steps = pl.num_programs(0)

    def gather(step, slot, action):
        base = step * tm

        def body(r, c):
            cp = pltpu.make_async_copy(ys_ref.at[pl.ds(pos_ref[base + r], 1)],
                                       buf_ref.at[slot, pl.ds(r, 1)], sems.at[slot])
            cp.start() if action == "start" else cp.wait()
            return c
        lax.fori_loop(0, tm, body, 0, unroll=DMA_UNROLL)

    slot = lax.rem(i, 2)

    @pl.when(i == 0)
    def _():
        gather(0, 0, "start")

    @pl.when(i + 1 < n_steps)
    def _():
        gather(i + 1, 1 - slot, "start")

    gather(i, slot, "wait")
    out_ref[...] = x1_ref[...] + buf_ref[slot]


def _combine(pos, ys, x1):
    n = x1.shape[0]
    tm = ROW_TILE
    return pl.pallas_call(
        _combine_kernel,
        grid_spec=pltpu.PrefetchScalarGridSpec(
            num_scalar_prefetch=1,
            grid=(n // tm,),
            in_specs=[
                pl.BlockSpec(memory_space=pl.ANY),
                pl.BlockSpec((tm, D_MODEL), lambda i, *_: (i, 0)),
            ],
            out_specs=pl.BlockSpec((tm, D_MODEL), lambda i, *_: (i, 0)),
            scratch_shapes=[
                pltpu.VMEM((2, tm, D_MODEL), F32),
                pltpu.SemaphoreType.DMA((2,)),
            ],
        ),
        out_shape=jax.ShapeDtypeStruct((n, D_MODEL), F32),
        compiler_params=_cparams("arbitrary"),
        name="combine",
    )(pos, ys, x1)


def _bucket_experts():
    ea, eb = [], []
    for g in range(N_GROUPS):
        for a in range(EXPERTS_PER_GROUP):
            for b in range(a + 1, EXPERTS_PER_GROUP):
                ea.append(g * EXPERTS_PER_GROUP + a)
                eb.append(g * EXPERTS_PER_GROUP + b)
    return jnp.asarray(ea, jnp.int32), jnp.asarray(eb, jnp.int32)


def _slot_plan(route_t, counts_row, n_tok):
    bucket = route_t[ROUTE_BUCKET].astype(jnp.int32)
    rank = route_t[ROUTE_RANK].astype(jnp.int32)
    buckets = jnp.arange(N_BUCKETS, dtype=jnp.int32)
    counts = counts_row[0, 0:N_BUCKETS].astype(jnp.int32)
    padded = (counts + MOE_BLOCK - 1) // MOE_BLOCK * MOE_BLOCK
    padded_ends = jnp.cumsum(padded)
    padded_starts = padded_ends - padded
    pos = jnp.sum(jnp.where(bucket[None, :] == buckets[:, None], padded_starts[:, None], 0), axis=0) + rank
    n_blocks = -(-n_tok // MOE_BLOCK) + N_BUCKETS
    n_valid = (padded_ends[-1] // MOE_BLOCK).astype(jnp.int32)
    block_start = jnp.arange(n_blocks, dtype=jnp.int32) * MOE_BLOCK
    block_start = jnp.minimum(block_start, padded_ends[-1] - MOE_BLOCK)
    block_bucket = jnp.sum((padded_ends[None, :] <= block_start[:, None]).astype(jnp.int32), axis=1)
    block_bucket = jnp.minimum(block_bucket, N_BUCKETS - 1)
    in_bucket = block_bucket[:, None] == buckets[None, :]
    ea, eb = _bucket_experts()
    return dict(pos=pos, padlo=(padded_starts + counts).astype(jnp.int32), padhi=padded_ends.astype(jnp.int32),
                block_ea=jnp.sum(jnp.where(in_bucket, ea[None, :], 0), axis=1),
                block_eb=jnp.sum(jnp.where(in_bucket, eb[None, :], 0), axis=1),
                n_valid=n_valid.reshape(1), n_slots=n_blocks * MOE_BLOCK)


def _block_diag(w):
    nb, r, c = w.shape
    eye = jnp.eye(nb, dtype=w.dtype)
    return (w[:, :, None, :] * eye[:, None, :, None]).reshape(nb * r, nb * c)


def kernel(x, norm1_g, w_in, q_norm_g, k_norm_g, lru_conv_w, lru_conv_b, lru_wa, lru_ba, lru_wx, lru_bx, lru_lambda, conf_dw_w, conf_dw_b, conf_ln_g, conf_ln_b, conf_pw_w, conf_pw_b, out_norm_g, w_out, norm2_g, router_coarse, router_fine, exp_w_gate, exp_w_up, exp_w_down):
    bsz, seq, d = x.shape
    n = bsz * seq
    depth = w_in.shape[0]
    assert d == D_MODEL and n % DENSE_TILE == 0 and n % ROW_TILE == 0
    assert seq % MIX_TILE == 0 and seq % (ATTN_TILES * ATTN_BQ) == 0

    heads = D_SB // HEAD_DIM
    head_of = jnp.arange(D_SB, dtype=jnp.int32) // HEAD_DIM
    gsum = (head_of[:, None] == head_of[None, :]).astype(BF16)
    t_io = jnp.arange(DENSE_TILE, dtype=jnp.int32)
    ltri = (t_io[None, :] < t_io[:, None]).astype(BF16)
    row = lambda v: v.reshape(1, -1)

    xf = x.reshape(n, d)
    for l in range(depth):
        qkv, rest = _inproj(xf, row(norm1_g[l]), w_in[l].astype(BF16),
                            row(jnp.tile(q_norm_g[l], heads)), row(jnp.tile(k_norm_g[l], heads)), gsum)
        ya = _attention(qkv, bsz, seq).reshape(n, D_SB)
        wgates = jnp.concatenate([_block_diag(lru_wa[l]), _block_diag(lru_wx[l])], axis=1).astype(BF16)
        ybc = _mixers(rest, bsz, seq, lru_conv_w[l], row(lru_conv_b[l]), wgates,
                      row(lru_ba[l]), row(lru_bx[l]), row(lru_lambda[l]),
                      conf_dw_w[l], row(conf_dw_b[l]), row(conf_ln_g[l]), row(conf_ln_b[l]),
                      conf_pw_w[l].astype(BF16), row(conf_pw_b[l])).reshape(n, D_LRU + D_CONF)
        wr = jnp.concatenate([router_coarse[l], router_fine[l]], axis=1)
        wr = jnp.pad(wr, ((0, 0), (0, LANES - wr.shape[1]))).astype(BF16)
        x1, h2, route_t, counts = _outproj(ya, ybc, xf, row(out_norm_g[l]), w_out[l].astype(BF16),
                                           row(norm2_g[l]), wr, ltri)
        plan = _slot_plan(route_t, counts, n)
        xs = _dispatch(plan["pos"], plan["padlo"], plan["padhi"], plan["n_valid"], h2, plan["n_slots"])
        ys = _experts(plan["block_ea"], plan["block_eb"], plan["n_valid"], xs,
                      exp_w_gate, exp_w_up, exp_w_down, l)
        xf = _combine(plan["pos"], ys, x1)
    return xf.reshape(bsz, seq, d)
```

```python
import functools

import jax
import jax.numpy as jnp
from jax import lax
from jax.experimental import pallas as pl
from jax.experimental.pallas import tpu as pltpu

F32 = jnp.float32
BF16 = jnp.bfloat16

D_MODEL = 1024
HEAD_DIM = 64
D_SB = 512
D_LRU = 256
LRU_BLOCKS = 4
LRU_CONV_WIDTH = 4
LRU_C = 8.0
D_CONF = 256
CONF_CONV_WIDTH = 31
D_QKV = 3 * D_SB
D_REST = 2 * D_LRU + 2 * D_CONF
N_GROUPS = 4
EXPERTS_PER_GROUP = 8
N_EXPERTS = N_GROUPS * EXPERTS_PER_GROUP
D_EXPERT = 512
D_PACKED = D_MODEL // 2
MOE_BLOCK = 256
EPS = 1e-6

LANES = 128
ROW_TILE = 1024
DENSE_TILE = 512
ATTN_BQ = 256
ATTN_BK = 128
ATTN_TILES = 8
MIX_TILE = 512
CONF_HALO = 32
DMA_UNROLL = 8
LRU_HALO = 8
ATTN_DEAD_LOG = -110.0
VMEM_LIMIT = 48 * 1024 * 1024
EXPERTS_VMEM_LIMIT = 56 * 1024 * 1024


def _cparams(*sem):
    return pltpu.CompilerParams(dimension_semantics=sem, vmem_limit_bytes=VMEM_LIMIT)


def _rms(y):
    return y * lax.rsqrt(jnp.mean(y * y, axis=-1, keepdims=True) + EPS)


def _pack_bf16_pairs(x):
    c = x.shape[1] // 2
    bits = pltpu.bitcast(x.astype(F32), jnp.uint32)
    return (bits[:, :c] >> 16) | (bits[:, c:] & jnp.uint32(0xFFFF0000))


def _unpack_bf16_pairs(words):
    low = pltpu.bitcast(words << 16, F32)
    high = pltpu.bitcast(words & jnp.uint32(0xFFFF0000), F32)
    return jnp.concatenate([low, high], axis=1).astype(BF16)


def _inproj_kernel(x_ref, g_ref, w_ref, qg_ref, kg_ref, gsum_ref, qkv_ref, rest_ref):
    h = (_rms(x_ref[...]) * g_ref[...]).astype(BF16)

    def head_norm(p, gain):
        s = jnp.dot((p * p).astype(BF16), gsum_ref[...], preferred_element_type=F32)
        return p * lax.rsqrt(s * (1.0 / HEAD_DIM) + EPS) * gain

    q = jnp.dot(h, w_ref[:, 0:D_SB], preferred_element_type=F32)
    qkv_ref[:, 0:D_SB] = head_norm(q, qg_ref[...]).astype(BF16)
    k = jnp.dot(h, w_ref[:, D_SB:2 * D_SB], preferred_element_type=F32)
    qkv_ref[:, D_SB:2 * D_SB] = head_norm(k, kg_ref[...]).astype(BF16)
    v = jnp.dot(h, w_ref[:, 2 * D_SB:D_QKV], preferred_element_type=F32)
    qkv_ref[:, 2 * D_SB:D_QKV] = v.astype(BF16)
    rest_ref[...] = jnp.dot(h, w_ref[:, D_QKV:], preferred_element_type=F32)


def _inproj(x, g, w, qg, kg, gsum):
    n = x.shape[0]
    tm = DENSE_TILE
    const = lambda i: (0, 0)
    return pl.pallas_call(
        _inproj_kernel,
        grid=(n // tm,),
        in_specs=[
            pl.BlockSpec((tm, D_MODEL), lambda i: (i, 0)),
            pl.BlockSpec((1, D_MODEL), const),
            pl.BlockSpec((D_MODEL, D_QKV + D_REST), const),
            pl.BlockSpec((1, D_SB), const),
            pl.BlockSpec((1, D_SB), const),
            pl.BlockSpec((D_SB, D_SB), const),
        ],
        out_specs=[
            pl.BlockSpec((tm, D_QKV), lambda i: (i, 0)),
            pl.BlockSpec((tm, D_REST), lambda i: (i, 0)),
        ],
        out_shape=[
            jax.ShapeDtypeStruct((n, D_QKV), BF16),
            jax.ShapeDtypeStruct((n, D_REST), F32),
        ],
        compiler_params=_cparams("arbitrary"),
        name="inproj",
    )(x, g, w, qg, kg, gsum)


def _attn_kernel(q_ref, k_ref, v_ref, o_ref, *scratch):
    def tile(tt, carry):
        rows = pl.ds(pl.multiple_of(tt * ATTN_BQ, ATTN_BQ), ATTN_BQ)
        _attn_tile(pl.program_id(2) * ATTN_TILES + tt, q_ref.at[0, rows], k_ref, v_ref, o_ref.at[0, rows], *scratch)
        return carry
    lax.fori_loop(0, ATTN_TILES, tile, 0)


def _attn_tile(i, q_ref, k_ref, v_ref, o_ref, z0, z1, lbc0, lbc1, hl0, hl1, c0, c1, cm0, cm1, w0, w1, acc_ref):
    bq, bk = ATTN_BQ, ATTN_BK
    heads = (0, 1)
    z_refs, lbc_refs, hilo_refs, c_refs, cmax_refs = (z0, z1), (lbc0, lbc1), (hl0, hl1), (c0, c1), (cm0, cm1)
    w_refs = (w0, w1)
    q = q_ref[...] * jnp.asarray(HEAD_DIM ** -0.5, BF16)
    first = lax.broadcasted_iota(jnp.int32, (bk, LANES), 1) < HEAD_DIM
    t_pos = lax.broadcasted_iota(jnp.int32, (bq, bk), 0) + i * bq
    s_io = lax.broadcasted_iota(jnp.int32, (bq, bk), 1)
    earlier = (lax.broadcasted_iota(jnp.int32, (bk, bk), 0)
               > lax.broadcasted_iota(jnp.int32, (bk, bk), 1))
    neg_later = jnp.where(earlier, -1.0, 0.0).astype(BF16)
    neg_later2 = jnp.concatenate([neg_later, neg_later], axis=0)

    def rows(m):
        return pl.ds(pl.multiple_of(jnp.maximum(m, 0) * bk, bk), bk)

    def causal(m):
        return (s_io + m * bk) < t_pos

    def per_head(x):
        zero = jnp.zeros_like(x)
        return jnp.concatenate([jnp.where(first, x, zero), jnp.where(first, zero, x)], axis=0)

    def stage0(m, s):
        z = lax.dot_general(q, per_head(k_ref[0, rows(m), :]), (((1,), (1,)), ((), ())),
                            preferred_element_type=F32)
        for h in heads:
            z_refs[s][h] = z[:, h * bk:(h + 1) * bk]

    def stage1(m, s, masked):
        mask = causal(m) if masked else None
        for h in heads:
            z = z_refs[s][h]
            c = c_refs[s][h]
            drop = jnp.maximum(z, 0.0) + jnp.log(1.0 + jnp.exp(-jnp.abs(z)))
            lbc_refs[s][h] = (z - drop) + c
            if masked:
                drop = jnp.where(mask, drop, 0.0)
            hi = drop.astype(BF16)
            lo = (drop - hi.astype(F32)).astype(BF16)
            hilo_refs[s][h] = jnp.concatenate([hi, lo], axis=1)
            c_next = c - jnp.sum(drop, axis=-1, keepdims=True)
            c_refs[1 - s][h] = c_next
            cmax_refs[1 - s][h] = jnp.max(c_next.reshape(bq // 8, 8, LANES), axis=0)

    def stage2_sums(s):
        return [jnp.dot(hilo_refs[s][h], neg_later2, preferred_element_type=F32) for h in heads]

    def stage2_apply(m, s, within, masked):
        mask = causal(m) if masked else None
        ws = []
        for h in heads:
            w = jnp.exp(lbc_refs[s][h] + within[h])
            if masked:
                w = jnp.where(mask, w, 0.0)
            ws.append(w.astype(BF16))
        w_refs[s][...] = jnp.concatenate(ws, axis=1)

    def stage3(m, s):
        acc_ref[...] += jnp.dot(w_refs[s][...], per_head(v_ref[0, rows(m), :]), preferred_element_type=F32)

    def alive(m, s):
        return jnp.logical_and(m >= 0, jnp.max(cmax_refs[s][...]) > ATTN_DEAD_LOG)

    top = (i + 1) * (bq // bk) - 1
    acc_ref[...] = jnp.zeros_like(acc_ref)
    c_refs[1][...] = jnp.zeros((2, bq, LANES), F32)

    def step(m, s, masked_next=False, masked=False, has_prev=True):
        live_next = alive(m - 1, 1 - s)
        if has_prev:
            stage3(m + 1, 1 - s)
        within = stage2_sums(s)
        stage0(m - 2, s)
        stage1(m - 1, 1 - s, masked_next)
        stage2_apply(m, s, within, masked)
        return live_next

    stage0(top, 1)
    stage0(top - 1, 0)
    stage1(top, 1, True)
    step(top, 1, masked_next=True, masked=True, has_prev=False)
    step(top - 1, 0, masked=True)

    def body(st):
        m, _ = st
        live_next = lax.cond((m & 1) == 0, functools.partial(step, m, 0), functools.partial(step, m, 1))
        return m - 1, live_next

    m_end, _ = lax.while_loop(lambda st: st[1], body, (top - 2, alive(top - 2, 1)))
    for s in (0, 1):
        pl.when(((m_end + 1) & 1) == s)(functools.partial(stage3, m_end + 1, s))
    o_ref[...] = acc_ref[...]


def _attention(qkv, bsz, seq):
    bq, bk = ATTN_BQ, ATTN_BK
    assert bq == 2 * bk
    pairs = D_SB // LANES
    qkv3 = qkv.reshape(bsz, seq, D_QKV)
    step_rows = ATTN_TILES * bq
    return pl.pallas_call(
        _attn_kernel,
        grid=(bsz, pairs, seq // step_rows),
        in_specs=[
            pl.BlockSpec((1, step_rows, LANES), lambda b, p, i: (b, i, p)),
            pl.BlockSpec((1, seq, LANES), lambda b, p, i: (b, 0, pairs + p)),
            pl.BlockSpec((1, seq, LANES), lambda b, p, i: (b, 0, 2 * pairs + p)),
        ],
        out_specs=pl.BlockSpec((1, step_rows, LANES), lambda b, p, i: (b, i, p)),
        out_shape=jax.ShapeDtypeStruct((bsz, seq, D_SB), F32),
        scratch_shapes=(
            [pltpu.VMEM((2, bq, bk), F32)] * 2
            + [pltpu.VMEM((2, bq, bk), F32)] * 2
            + [pltpu.VMEM((2, bq, 2 * bk), BF16)] * 2
            + [pltpu.VMEM((2, bq, LANES), F32)] * 2
            + [pltpu.VMEM((2, 8, LANES), F32)] * 2
            + [pltpu.VMEM((bq, 2 * bk), BF16)] * 2
            + [pltpu.VMEM((bq, LANES), F32)]
        ),
        compiler_params=_cparams("arbitrary", "arbitrary", "arbitrary"),
        name="attn",
    )(qkv3, qkv3, qkv3)


def _shift_rows(x, d, fill):
    t = x.shape[0]
    if d % 8 == 0:
        return jnp.concatenate([jnp.full((d, x.shape[1]), fill, x.dtype), x[:t - d]], axis=0)
    rows = lax.broadcasted_iota(jnp.int32, x.shape, 0)
    return jnp.where(rows < d, fill, pltpu.roll(x, d, axis=0))


def _mixers_kernel(rest_ref, lcw_ref, lcb_ref, wg_ref, ba_ref, bx_ref, lam_ref,
                   cw_ref, cb_ref, lng_ref, lnb_ref, pw_ref, pb_ref,
                   y_ref, ltail_ref, h_ref, ctail_ref, lwin_ref, cwin_ref):
    t = MIX_TILE

    @pl.when(pl.program_id(1) == 0)
    def _():
        ltail_ref[...] = jnp.zeros_like(ltail_ref)
        h_ref[...] = jnp.zeros_like(h_ref)
        ctail_ref[...] = jnp.zeros_like(ctail_ref)

    xr = rest_ref[0, :, 0:D_LRU]
    xg = rest_ref[0, :, D_LRU:2 * D_LRU]
    val = rest_ref[0, :, 2 * D_LRU:2 * D_LRU + D_CONF]
    gate = rest_ref[0, :, 2 * D_LRU + D_CONF:]

    lwin_ref[0:LRU_HALO, :] = ltail_ref[...]
    lwin_ref[LRU_HALO:, :] = xr
    ltail_ref[...] = xr[t - LRU_HALO:, :]
    conv = lcb_ref[...]
    for k in range(LRU_CONV_WIDTH):
        off = LRU_HALO - (LRU_CONV_WIDTH - 1) + k
        conv = conv + lcw_ref[k:k + 1, :] * lwin_ref[off:off + t, :]
    gates = jnp.dot(conv.astype(BF16), wg_ref[...], preferred_element_type=F32)
    gate_r = gates[:, 0:D_LRU] + ba_ref[...]
    gate_i = gates[:, D_LRU:] + bx_ref[...]
    log_a = LRU_C * jax.nn.sigmoid(gate_r) * jax.nn.log_sigmoid(lam_ref[...])
    a = jnp.exp(log_a)
    th = jnp.tanh(log_a)
    u = jnp.sqrt(-2.0 * th / (1.0 - th)) * (jax.nn.sigmoid(gate_i) * conv)
    d = 1
    while d < t:
        u = a * _shift_rows(u, d, 0.0) + u
        a = a * _shift_rows(a, d, 1.0)
        d *= 2
    h = a * h_ref[...] + u
    h_ref[...] = h[t - 1:t, :]
    y_ref[0, :, 0:D_LRU] = h * jax.nn.gelu(xg)

    glu = val * jax.nn.sigmoid(gate)
    cwin_ref[0:CONF_HALO, :] = ctail_ref[...]
    cwin_ref[CONF_HALO:, :] = glu
    ctail_ref[...] = glu[t - CONF_HALO:, :]
    acc = cb_ref[...]
    first_off = CONF_HALO - (CONF_CONV_WIDTH - 1)
    for r in range(8):
        offs = [o for o in range(first_off, first_off + CONF_CONV_WIDTH) if o % 8 == r]
        span = t + (8 if r else 0)
        part = None
        for o in offs:
            term = cw_ref[o - first_off:o - first_off + 1, :] * cwin_ref[o - r:o - r + span, :]
            part = term if part is None else part + term
        acc = acc + part[r:r + t, :]
    mu = jnp.mean(acc, axis=-1, keepdims=True)
    cen = acc - mu
    var = jnp.mean(cen * cen, axis=-1, keepdims=True)
    ln = cen * lax.rsqrt(var + EPS) * lng_ref[...] + lnb_ref[...]
    sw = (ln * jax.nn.sigmoid(ln)).astype(BF16)
    y_ref[0, :, D_LRU:] = jnp.dot(sw, pw_ref[...], preferred_element_type=F32) + pb_ref[...]


def _mixers(rest, bsz, seq, lcw, lcb, wgates, ba, bx, lam, cw, cb, lng, lnb, pw, pb):
    t = MIX_TILE
    rest3 = rest.reshape(bsz, seq, D_REST)
    const = lambda b, i: (0, 0)
    row = lambda d: pl.BlockSpec((1, d), const)
    return pl.pallas_call(
        _mixers_kernel,
        grid=(bsz, seq // t),
        in_specs=[
            pl.BlockSpec((1, t, D_REST), lambda b, i: (b, i, 0)),
            pl.BlockSpec((LRU_CONV_WIDTH, D_LRU), const), row(D_LRU),
            pl.BlockSpec((D_LRU, 2 * D_LRU), const), row(D_LRU), row(D_LRU), row(D_LRU),
            pl.BlockSpec((CONF_CONV_WIDTH, D_CONF), const), row(D_CONF),
            row(D_CONF), row(D_CONF),
            pl.BlockSpec((D_CONF, D_CONF), const), row(D_CONF),
        ],
        out_specs=pl.BlockSpec((1, t, D_LRU + D_CONF), lambda b, i: (b, i, 0)),
        out_shape=jax.ShapeDtypeStruct((bsz, seq, D_LRU + D_CONF), F32),
        scratch_shapes=[
            pltpu.VMEM((LRU_HALO, D_LRU), F32),
            pltpu.VMEM((1, D_LRU), F32),
            pltpu.VMEM((CONF_HALO, D_CONF), F32),
            pltpu.VMEM((LRU_HALO + t, D_LRU), F32),
            pltpu.VMEM((CONF_HALO + t, D_CONF), F32),
        ],
        compiler_params=_cparams("arbitrary", "arbitrary"),
        name="mixers",
    )(rest3, lcw, lcb, wgates, ba, bx, lam, cw, cb, lng, lnb, pw, pb)


ROUTE_GATE_A, ROUTE_GATE_B, ROUTE_BUCKET, ROUTE_RANK = range(4)
ROUTE_ROWS = 8
PAIRS_PER_GROUP = EXPERTS_PER_GROUP * (EXPERTS_PER_GROUP - 1) // 2
N_BUCKETS = N_GROUPS * PAIRS_PER_GROUP
assert N_BUCKETS <= LANES
D_ROUTED = D_PACKED + LANES
ROUTER_FINE_COL0 = N_GROUPS


def _outproj_kernel(ya_ref, ybc_ref, x_ref, gout_ref, wout_ref, g2_ref, wr_ref, ltri_ref,
                    x1_ref, h2_ref, route_t_ref, counts_ref, cnt_ref):
    @pl.when(pl.program_id(0) == 0)
    def _():
        cnt_ref[...] = jnp.zeros_like(cnt_ref)

    y = jnp.concatenate(
        [_rms(ya_ref[...]), _rms(ybc_ref[:, 0:D_LRU]), _rms(ybc_ref[:, D_LRU:])], axis=1)
    y = (y * gout_ref[...]).astype(BF16)
    x1 = x_ref[...] + jnp.dot(y, wout_ref[...], preferred_element_type=F32)
    x1_ref[...] = x1
    h2 = (_rms(x1) * g2_ref[...]).astype(BF16)
    h2_ref[:, 0:D_PACKED] = _pack_bf16_pairs(h2)
    lf = jnp.dot(h2, wr_ref[...], preferred_element_type=F32)

    tm = lf.shape[0]
    col_i = lax.broadcasted_iota(jnp.int32, (tm, LANES), 1)
    col = col_i.astype(F32)
    neg = -jnp.inf
    big = float(LANES)
    is_c = col_i < N_GROUPS
    lc = jnp.where(is_c, lf, neg)
    mc = jnp.max(lc, axis=-1, keepdims=True)
    grp = jnp.min(jnp.where(lc == mc, col, big), axis=-1, keepdims=True)
    w_grp = 1.0 / jnp.sum(jnp.where(is_c, jnp.exp(lf - mc), 0.0), axis=-1, keepdims=True)
    lo = ROUTER_FINE_COL0 + EXPERTS_PER_GROUP * grp
    in_grp = jnp.logical_and(col >= lo, col < lo + EXPERTS_PER_GROUP)
    l1 = jnp.where(in_grp, lf, neg)
    v1 = jnp.max(l1, axis=-1, keepdims=True)
    i1 = jnp.min(jnp.where(l1 == v1, col, big), axis=-1, keepdims=True)
    sel1 = col == i1
    l2 = jnp.where(sel1, neg, l1)
    v2 = jnp.max(l2, axis=-1, keepdims=True)
    i2 = jnp.min(jnp.where(l2 == v2, col, big), axis=-1, keepdims=True)
    sel2 = col == i2
    e21 = jnp.exp(v2 - v1)
    g1 = w_grp / (1.0 + e21)
    g2 = w_grp * e21 / (1.0 + e21)

    swap = i2 < i1
    a = jnp.where(swap, i2, i1) - lo
    b = jnp.where(swap, i1, i2) - lo
    gate_a = jnp.where(swap, g2, g1)
    gate_b = jnp.where(swap, g1, g2)
    pair = a * (2 * EXPERTS_PER_GROUP - 1 - a) * 0.5 + (b - a - 1.0)
    bucket = grp * PAIRS_PER_GROUP + pair

    sel = col == bucket
    onehot = jnp.where(sel, 1.0, 0.0)
    before = jnp.dot(ltri_ref[...], onehot.astype(BF16), preferred_element_type=F32)
    cnt = cnt_ref[...]
    rank = jnp.sum(jnp.where(sel, before + cnt, 0.0), axis=-1, keepdims=True)
    cnt_ref[...] = cnt + jnp.sum(onehot, axis=0, keepdims=True)

    out = jnp.zeros((tm, LANES), F32)
    for lane_id, val in ((ROUTE_GATE_A, gate_a), (ROUTE_GATE_B, gate_b), (ROUTE_BUCKET, bucket), (ROUTE_RANK, rank)):
        out = jnp.where(col_i == lane_id, val, out)
    h2_ref[:, D_PACKED:] = pltpu.bitcast(out, jnp.uint32)
    route_t_ref[...] = out.T[0:ROUTE_ROWS, :]
    counts_ref[...] = cnt_ref[...]


def _outproj(ya, ybc, x, gout, wout, g2, wr, ltri):
    n = x.shape[0]
    tm = DENSE_TILE
    const = lambda i: (0, 0)
    tile = lambda d: pl.BlockSpec((tm, d), lambda i: (i, 0))
    return pl.pallas_call(
        _outproj_kernel,
        grid=(n // tm,),
        in_specs=[
            tile(D_SB), tile(D_LRU + D_CONF), tile(D_MODEL),
            pl.BlockSpec((1, D_MODEL), const),
            pl.BlockSpec((D_MODEL, D_MODEL), const),
            pl.BlockSpec((1, D_MODEL), const),
            pl.BlockSpec((D_MODEL, LANES), const),
            pl.BlockSpec((tm, tm), const),
        ],
        out_specs=[tile(D_MODEL), tile(D_ROUTED),
                   pl.BlockSpec((ROUTE_ROWS, tm), lambda i: (0, i)),
                   pl.BlockSpec((1, LANES), const)],
        out_shape=[
            jax.ShapeDtypeStruct((n, D_MODEL), F32),
            jax.ShapeDtypeStruct((n, D_ROUTED), jnp.uint32),
            jax.ShapeDtypeStruct((ROUTE_ROWS, n), F32),
            jax.ShapeDtypeStruct((1, LANES), F32),
        ],
        scratch_shapes=[pltpu.VMEM((1, LANES), F32)],
        compiler_params=_cparams("arbitrary"),
        name="outproj",
    )(ya, ybc, x, gout, wout, g2, wr, ltri)


def _row_copy(src_ref, src_row, dst_ref, dst_row, sem):
    return pltpu.make_async_copy(src_ref.at[pl.ds(src_row, 1)], dst_ref.at[pl.ds(dst_row, 1)], sem)


ZERO_CHUNK_ROWS = (128, 64, 32, 16, 8)


def _dispatch_kernel(pos_ref, padlo_ref, padhi_ref, nv_ref, h2_ref, xs_ref, zero_ref, sem, zsem):
    tm = ROW_TILE
    i = pl.program_id(0)
    base = i * tm

    @pl.when(i == 0)
    def _():
        zero_ref[...] = jnp.zeros_like(zero_ref)

        def fill_tail(action, e, carry):
            def go(copy):
                copy.start() if action == "start" else copy.wait()

            lo, hi = padlo_ref[e], padhi_ref[e]
            lo8 = jnp.minimum((lo + 7) // 8 * 8, hi)

            def one_row(r, c):
                go(_row_copy(zero_ref, 0, xs_ref, r, zsem))
                return c
            lax.fori_loop(lo, lo8, one_row, 0)
            groups = (hi - lo8) // 8
            cur = lo8
            for rows in ZERO_CHUNK_ROWS:
                take = (groups & (rows // 8)) != 0

                @pl.when(take)
                def _(rows=rows, cur=cur):
                    dst = xs_ref.at[pl.ds(pl.multiple_of(cur, 8), rows)]
                    go(pltpu.make_async_copy(zero_ref.at[pl.ds(0, rows)], dst, zsem))
                cur = cur + jnp.where(take, rows, 0)
            return carry

        def idle_block(action, b, carry):
            rows = pl.ds(pl.multiple_of(b * MOE_BLOCK, MOE_BLOCK), MOE_BLOCK)
            copy = pltpu.make_async_copy(zero_ref, xs_ref.at[rows], zsem)
            copy.start() if action == "start" else copy.wait()
            return carry

        n_blocks = xs_ref.shape[0] // MOE_BLOCK
        for action in ("start", "wait"):
            lax.fori_loop(0, N_BUCKETS, functools.partial(fill_tail, action), 0)
            lax.fori_loop(nv_ref[0], n_blocks, functools.partial(idle_block, action), 0)

    def start(r2, c):
        for queue in (0, 1):
            r = 2 * r2 + queue
            _row_copy(h2_ref, r, xs_ref, pos_ref[base + r], sem).start(priority=queue)
        return c
    lax.fori_loop(0, tm // 2, start, 0, unroll=DMA_UNROLL // 2)

    def wait(r, c):
        _row_copy(h2_ref, r, xs_ref, pos_ref[base + r], sem).wait()
        return c
    lax.fori_loop(0, tm, wait, 0, unroll=DMA_UNROLL)


def _dispatch(pos, padlo, padhi, n_valid, h2, n_slots):
    n = h2.shape[0]
    tm = ROW_TILE
    assert sum(ZERO_CHUNK_ROWS) + 8 == MOE_BLOCK
    return pl.pallas_call(
        _dispatch_kernel,
        grid_spec=pltpu.PrefetchScalarGridSpec(
            num_scalar_prefetch=4,
            grid=(n // tm,),
            in_specs=[pl.BlockSpec((tm, D_ROUTED), lambda i, *_: (i, 0))],
            out_specs=pl.BlockSpec(memory_space=pl.ANY),
            scratch_shapes=[
                pltpu.VMEM((MOE_BLOCK, D_ROUTED), jnp.uint32),
                pltpu.SemaphoreType.DMA(()),
                pltpu.SemaphoreType.DMA(()),
            ],
        ),
        out_shape=jax.ShapeDtypeStruct((n_slots, D_ROUTED), jnp.uint32),
        compiler_params=_cparams("arbitrary"),
        name="dispatch",
    )(pos, padlo, padhi, n_valid, h2)


def _experts_kernel(ea_ref, eb_ref, nv_ref, xs_ref, wg_hbm, wu_hbm, wd_hbm, ys_ref,
                    wg_bf, wu_bf, wd_bf, wg_stage, wu_stage, wd_stage, sems, *, layer):
    i = pl.program_id(0)
    live = i < nv_ref[0]
    group = ea_ref[i] // EXPERTS_PER_GROUP
    new_group = jnp.logical_or(i == 0, group != ea_ref[jnp.maximum(i - 1, 0)] // EXPERTS_PER_GROUP)
    hbm_stage_cache = ((wg_hbm, wg_stage, wg_bf), (wu_hbm, wu_stage, wu_bf), (wd_hbm, wd_stage, wd_bf))

    @pl.when(jnp.logical_and(live, new_group))
    def _():
        def fetch(j, action):
            for k, (hbm, stage, _) in enumerate(hbm_stage_cache):
                copy = pltpu.make_async_copy(hbm.at[layer, group * EXPERTS_PER_GROUP + j],
                                             stage.at[j % 2], sems.at[j % 2, k])
                copy.start() if action == "start" else copy.wait()

        fetch(0, "start")
        for j in range(EXPERTS_PER_GROUP):
            if j + 1 < EXPERTS_PER_GROUP:
                fetch(j + 1, "start")
            fetch(j, "wait")
            for _, stage, cache in hbm_stage_cache:
                cache[j] = stage[j % 2].astype(BF16)

    @pl.when(live)
    def _():
        x = _unpack_bf16_pairs(xs_ref[:, 0:D_PACKED])
        record = pltpu.bitcast(xs_ref[:, D_PACKED:], F32)
        lane = lax.broadcasted_iota(jnp.int32, record.shape, 1)
        y = None
        for gate_lane, e_ref in ((ROUTE_GATE_A, ea_ref), (ROUTE_GATE_B, eb_ref)):
            e = e_ref[i] % EXPERTS_PER_GROUP
            gate = jnp.sum(jnp.where(lane == gate_lane, record, 0.0), axis=-1, keepdims=True)
            g = jnp.dot(x, wg_bf[e], preferred_element_type=F32)
            u = jnp.dot(x, wu_bf[e], preferred_element_type=F32)
            act = (g * jax.nn.sigmoid(g) * u).astype(BF16)
            part = jnp.dot(act, wd_bf[e], preferred_element_type=F32) * gate
            y = part if y is None else y + part
        ys_ref[...] = y

    @pl.when(jnp.logical_not(live))
    def _():
        ys_ref[...] = jnp.zeros_like(ys_ref)


def _experts(block_ea, block_eb, n_valid, xs, wg, wu, wd, layer):
    n_slots = xs.shape[0]
    n_blocks = n_slots // MOE_BLOCK
    hbm = pl.BlockSpec(memory_space=pl.ANY)
    return pl.pallas_call(
        functools.partial(_experts_kernel, layer=layer),
        grid_spec=pltpu.PrefetchScalarGridSpec(
            num_scalar_prefetch=3,
            grid=(n_blocks,),
            in_specs=[
                pl.BlockSpec((MOE_BLOCK, D_ROUTED), lambda i, ea, eb, nv: (jnp.minimum(i, nv[0] - 1), 0)),
                hbm, hbm, hbm,
            ],
            out_specs=pl.BlockSpec((MOE_BLOCK, D_MODEL), lambda i, ea, eb, nv: (i, 0)),
            scratch_shapes=[
                pltpu.VMEM((EXPERTS_PER_GROUP, D_MODEL, D_EXPERT), BF16),
                pltpu.VMEM((EXPERTS_PER_GROUP, D_MODEL, D_EXPERT), BF16),
                pltpu.VMEM((EXPERTS_PER_GROUP, D_EXPERT, D_MODEL), BF16),
                pltpu.VMEM((2, D_MODEL, D_EXPERT), F32),
                pltpu.VMEM((2, D_MODEL, D_EXPERT), F32),
                pltpu.VMEM((2, D_EXPERT, D_MODEL), F32),
                pltpu.SemaphoreType.DMA((2, 3)),
            ],
        ),
        out_shape=jax.ShapeDtypeStruct((n_slots, D_MODEL), F32),
        compiler_params=pltpu.CompilerParams(dimension_semantics=("arbitrary",),
                                             vmem_limit_bytes=EXPERTS_VMEM_LIMIT),
        name="experts",
    )(block_ea, block_eb, n_valid, xs, wg, wu, wd)


def _combine_kernel(pos_ref, ys_ref, x1_ref, out_ref, buf_ref, sems):
    tm = ROW_TILE
    i = pl.program_id(0)
    n_steps = pl.num_programs(0)

    def gather(step, slot, action):
        base = step * tm

        def body(r2, c):
            for queue in (0, 1):
                r = 2 * r2 + queue
                cp = pltpu.make_async_copy(ys_ref.at[pl.ds(pos_ref[base + r], 1)],
                                           buf_ref.at[slot, pl.ds(r, 1)], sems.at[slot])
                cp.start(priority=queue) if action == "start" else cp.wait()
            return c
        lax.fori_loop(0, tm // 2, body, 0, unroll=DMA_UNROLL // 2)

    slot = lax.rem(i, 2)

    @pl.when(i == 0)
    def _():
        gather(0, 0, "start")

    @pl.when(i + 1 < n_steps)
    def _():
        gather(i + 1, 1 - slot, "start")

    gather(i, slot, "wait")
    out_ref[...] = x1_ref[...] + buf_ref[slot]


def _combine(pos, ys, x1):
    n = x1.shape[0]
    tm = ROW_TILE
    return pl.pallas_call(
        _combine_kernel,
        grid_spec=pltpu.PrefetchScalarGridSpec(
            num_scalar_prefetch=1,
            grid=(n // tm,),
            in_specs=[
                pl.BlockSpec(memory_space=pl.ANY),
                pl.BlockSpec((tm, D_MODEL), lambda i, *_: (i, 0)),
            ],
            out_specs=pl.BlockSpec((tm, D_MODEL), lambda i, *_: (i, 0)),
            scratch_shapes=[
                pltpu.VMEM((2, tm, D_MODEL), F32),
                pltpu.SemaphoreType.DMA((2,)),
            ],
        ),
        out_shape=jax.ShapeDtypeStruct((n, D_MODEL), F32),
        compiler_params=_cparams("arbitrary"),
        name="combine",
    )(pos, ys, x1)


def _bucket_experts():
    ea, eb = [], []
    for g in range(N_GROUPS):
        for a in range(EXPERTS_PER_GROUP):
            for b in range(a + 1, EXPERTS_PER_GROUP):
                ea.append(g * EXPERTS_PER_GROUP + a)
                eb.append(g * EXPERTS_PER_GROUP + b)
    return jnp.asarray(ea, jnp.int32), jnp.asarray(eb, jnp.int32)


def _slot_plan(route_t, counts_row, n_tok):
    bucket = route_t[ROUTE_BUCKET].astype(jnp.int32)
    rank = route_t[ROUTE_RANK].astype(jnp.int32)
    buckets = jnp.arange(N_BUCKETS, dtype=jnp.int32)
    counts = counts_row[0, 0:N_BUCKETS].astype(jnp.int32)
    padded = (counts + MOE_BLOCK - 1) // MOE_BLOCK * MOE_BLOCK
    padded_ends = jnp.cumsum(padded)
    padded_starts = padded_ends - padded
    pos = jnp.sum(jnp.where(bucket[None, :] == buckets[:, None], padded_starts[:, None], 0), axis=0) + rank
    n_blocks = -(-n_tok // MOE_BLOCK) + N_BUCKETS
    n_valid = (padded_ends[-1] // MOE_BLOCK).astype(jnp.int32)
    block_start = jnp.arange(n_blocks, dtype=jnp.int32) * MOE_BLOCK
    block_start = jnp.minimum(block_start, padded_ends[-1] - MOE_BLOCK)
    block_bucket = jnp.sum((padded_ends[None, :] <= block_start[:, None]).astype(jnp.int32), axis=1)
    block_bucket = jnp.minimum(block_bucket, N_BUCKETS - 1)
    in_bucket = block_bucket[:, None] == buckets[None, :]
    ea, eb = _bucket_experts()
    return dict(pos=pos, padlo=(padded_starts + counts).astype(jnp.int32), padhi=padded_ends.astype(jnp.int32),
                block_ea=jnp.sum(jnp.where(in_bucket, ea[None, :], 0), axis=1),
                block_eb=jnp.sum(jnp.where(in_bucket, eb[None, :], 0), axis=1),
                n_valid=n_valid.reshape(1), n_slots=n_blocks * MOE_BLOCK)


def _block_diag(w):
    nb, r, c = w.shape
    eye = jnp.eye(nb, dtype=w.dtype)
    return (w[:, :, None, :] * eye[:, None, :, None]).reshape(nb * r, nb * c)


def kernel(x, norm1_g, w_in, q_norm_g, k_norm_g, lru_conv_w, lru_conv_b, lru_wa, lru_ba, lru_wx, lru_bx, lru_lambda, conf_dw_w, conf_dw_b, conf_ln_g, conf_ln_b, conf_pw_w, conf_pw_b, out_norm_g, w_out, norm2_g, router_coarse, router_fine, exp_w_gate, exp_w_up, exp_w_down):
    bsz, seq, d = x.shape
    n = bsz * seq
    depth = w_in.shape[0]
    assert d == D_MODEL and n % DENSE_TILE == 0 and n % ROW_TILE == 0
    assert seq % MIX_TILE == 0 and seq % (ATTN_TILES * ATTN_BQ) == 0

    heads = D_SB // HEAD_DIM
    head_of = jnp.arange(D_SB, dtype=jnp.int32) // HEAD_DIM
    gsum = (head_of[:, None] == head_of[None, :]).astype(BF16)
    t_io = jnp.arange(DENSE_TILE, dtype=jnp.int32)
    ltri = (t_io[None, :] < t_io[:, None]).astype(BF16)
    row = lambda v: v.reshape(1, -1)

    xf = x.reshape(n, d)
    for l in range(depth):
        qkv, rest = _inproj(xf, row(norm1_g[l]), w_in[l].astype(BF16),
                            row(jnp.tile(q_norm_g[l], heads)), row(jnp.tile(k_norm_g[l], heads)), gsum)
        ya = _attention(qkv, bsz, seq).reshape(n, D_SB)
        wgates = jnp.concatenate([_block_diag(lru_wa[l]), _block_diag(lru_wx[l])], axis=1).astype(BF16)
        ybc = _mixers(rest, bsz, seq, lru_conv_w[l], row(lru_conv_b[l]), wgates,
                      row(lru_ba[l]), row(lru_bx[l]), row(lru_lambda[l]),
                      conf_dw_w[l], row(conf_dw_b[l]), row(conf_ln_g[l]), row(conf_ln_b[l]),
                      conf_pw_w[l].astype(BF16), row(conf_pw_b[l])).reshape(n, D_LRU + D_CONF)
        wr = jnp.concatenate([router_coarse[l], router_fine[l]], axis=1)
        wr = jnp.pad(wr, ((0, 0), (0, LANES - wr.shape[1]))).astype(BF16)
        x1, h2, route_t, counts = _outproj(ya, ybc, xf, row(out_norm_g[l]), w_out[l].astype(BF16),
                                           row(norm2_g[l]), wr, ltri)
        plan = _slot_plan(route_t, counts, n)
        xs = _dispatch(plan["pos"], plan["padlo"], plan["padhi"], plan["n_valid"], h2, plan["n_slots"])
        ys = _experts(plan["block_ea"], plan["block_eb"], plan["n_valid"], xs,
                      exp_w_gate, exp_w_up, exp_w_down, l)
        xf = _combine(plan["pos"], ys, x1)
    return xf.reshape(bsz, seq, d)
```
